```python
import math
import jax, jax.numpy as jnp
from jax import lax
import numpy as np

D_MODEL = 2048
BATCH = 16
SEQ = 256
DEPTH = 4
DEC_BATCH = 2
DEC_SEQ = 2048
PAST_LEN = 256

GRID_W = 64
N_MIXERS = 2
N_WIN_LAYERS = (DEPTH + 1) // 2
N_MLA_LAYERS = DEPTH // 2
WIN_HEADS = 32
WIN_KV_HEADS = 8
WIN_GROUP = WIN_HEADS // WIN_KV_HEADS
WIN_HEAD_DIM = 64
WINDOW = 128
BLOCK = WINDOW
WIN_SCALE = WIN_HEAD_DIM ** -0.5
MLA_HEADS = 16
Q_LORA_RANK = 512
KV_LORA_RANK = 512
QK_NOPE_DIM = 128
QK_ROPE_DIM = 64
V_HEAD_DIM = 128
MLA_SCALE = (QK_NOPE_DIM + QK_ROPE_DIM) ** -0.5
D_FF = -(-8 * D_MODEL // (3 * 256)) * 256
ROPE_BASE = 10000.0
EPS = 1e-6
NEG = float(np.finfo(np.float32).min)

kernel_name = "hybrid_dit_prefix_window_mla_step"


def rms_norm(x, g):
    xf = x.astype(jnp.float32)
    y = xf * lax.rsqrt(jnp.mean(xf * xf, axis=-1, keepdims=True) + EPS)
    return (y * g.astype(jnp.float32)).astype(x.dtype)


def ada_modulation(cond, w_ada, b_ada):
    m = jax.nn.silu(cond) @ w_ada + b_ada
    return jnp.split(m[:, None, :], 6, axis=-1)


def modulate(x, g, shift, scale):
    return rms_norm(x, g) * (1 + scale) + shift


def grid_positions(n_tokens):
    n_rows = n_tokens // GRID_W
    rows = jnp.broadcast_to(jnp.arange(n_rows, dtype=jnp.int32)[:, None], (n_rows, GRID_W)).reshape(-1)
    cols = jnp.broadcast_to(jnp.arange(GRID_W, dtype=jnp.int32)[None, :], (n_rows, GRID_W)).reshape(-1)
    return rows, cols


def rope_angles(pos, dim):
    half = dim // 2
    freqs = ROPE_BASE ** (-jnp.arange(half, dtype=jnp.float32) / half)
    ang = pos.astype(jnp.float32)[:, None] * freqs[None, :]
    return jnp.cos(ang), jnp.sin(ang)


def rope_1d(x, cos, sin):
    half = x.shape[-1] // 2
    x1, x2 = x[..., :half], x[..., half:]
    c = cos[:, None, :].astype(x.dtype)
    s = sin[:, None, :].astype(x.dtype)
    return jnp.concatenate([x1 * c - x2 * s, x1 * s + x2 * c], axis=-1)


def rope_2d(x, rows, cols):
    half = x.shape[-1] // 2
    cr, sr = rope_angles(rows, half)
    cc, sc = rope_angles(cols, half)
    return jnp.concatenate([rope_1d(x[..., :half], cr, sr), rope_1d(x[..., half:], cc, sc)], axis=-1)


def sink_softmax(logits, sink_logit):
    sink = jnp.broadcast_to(sink_logit, logits.shape[:-1] + (1,))
    p = jax.nn.softmax(jnp.concatenate([logits, sink], axis=-1), axis=-1)
    return p[..., :-1]


def win_qkv(h, w_qkv):
    B, L, _ = h.shape
    qd, kd = WIN_HEADS * WIN_HEAD_DIM, WIN_KV_HEADS * WIN_HEAD_DIM
    q, k, v = jnp.split(h @ w_qkv, [qd, qd + kd], axis=-1)
    return (q.reshape(B, L, WIN_HEADS, WIN_HEAD_DIM),
            k.reshape(B, L, WIN_KV_HEADS, WIN_HEAD_DIM),
            v.reshape(B, L, WIN_KV_HEADS, WIN_HEAD_DIM))


def win_attn_context(h, w_qkv, w_o, sink):
    B, L, _ = h.shape
    q, k, v = win_qkv(h, w_qkv)
    q = q.reshape(B, L, WIN_KV_HEADS, WIN_GROUP, WIN_HEAD_DIM)
    logits = jnp.einsum('bqkgd,bskd->bkgqs', q, k).astype(jnp.float32) * WIN_SCALE
    p = sink_softmax(logits, sink.reshape(WIN_KV_HEADS, WIN_GROUP)[None, :, :, None, None].astype(jnp.float32))
    o = jnp.einsum('bkgqs,bskd->bqkgd', p.astype(v.dtype), v).reshape(B, L, WIN_HEADS * WIN_HEAD_DIM)
    return o @ w_o, k, v


def win_attn_latent(h, ctx_k, ctx_v, rows, cols, w_qkv, w_o, sink):
    B, L, _ = h.shape
    nb = L // BLOCK
    q, k, v = win_qkv(h, w_qkv)
    q = rope_2d(q, rows, cols).reshape(B, nb, BLOCK, WIN_KV_HEADS, WIN_GROUP, WIN_HEAD_DIM)
    k = rope_2d(k, rows, cols)
    pad = ((0, 0), (WINDOW, WINDOW), (0, 0), (0, 0))
    kb = jnp.pad(k, pad).reshape(B, nb + 2, BLOCK, WIN_KV_HEADS, WIN_HEAD_DIM)
    vb = jnp.pad(v, pad).reshape(B, nb + 2, BLOCK, WIN_KV_HEADS, WIN_HEAD_DIM)
    kw = jnp.concatenate([kb[:, :-2], kb[:, 1:-1], kb[:, 2:]], axis=2)
    vw = jnp.concatenate([vb[:, :-2], vb[:, 1:-1], vb[:, 2:]], axis=2)
    blk_start = (jnp.arange(nb) * BLOCK)[:, None, None]
    key_pos = blk_start - WINDOW + jnp.arange(3 * BLOCK)[None, None, :]
    qry_pos = blk_start + jnp.arange(BLOCK)[None, :, None]
    valid = (jnp.abs(key_pos - qry_pos) <= WINDOW) & (key_pos >= 0) & (key_pos < L)
    lw = jnp.einsum('bnqkgd,bnskd->bnkgqs', q, kw).astype(jnp.float32) * WIN_SCALE
    lw = jnp.where(valid[None, :, None, None, :, :], lw, NEG)
    lc = jnp.einsum('bnqkgd,bckd->bnkgqc', q, ctx_k).astype(jnp.float32) * WIN_SCALE
    p = sink_softmax(jnp.concatenate([lw, lc], axis=-1),
                     sink.reshape(WIN_KV_HEADS, WIN_GROUP)[None, None, :, :, None, None].astype(jnp.float32))
    pw = p[..., :3 * BLOCK].astype(v.dtype)
    pc = p[..., 3 * BLOCK:].astype(v.dtype)
    o = (jnp.einsum('bnkgqs,bnskd->bnqkgd', pw, vw)
         + jnp.einsum('bnkgqc,bckd->bnqkgd', pc, ctx_v))
    return o.reshape(B, L, WIN_HEADS * WIN_HEAD_DIM) @ w_o


def mla_project(h, w_in, q_norm, w_q_b, kv_norm):
    B, L, _ = h.shape
    c_q, c_kv, k_pe = jnp.split(h @ w_in, [Q_LORA_RANK, Q_LORA_RANK + KV_LORA_RANK], axis=-1)
    q = (rms_norm(c_q, q_norm) @ w_q_b).reshape(B, L, MLA_HEADS, QK_NOPE_DIM + QK_ROPE_DIM)
    return q[..., :QK_NOPE_DIM], q[..., QK_NOPE_DIM:], rms_norm(c_kv, kv_norm), k_pe


def mla_expand(c_kv, w_kv_b):
    B, L, _ = c_kv.shape
    kv = (c_kv @ w_kv_b).reshape(B, L, MLA_HEADS, QK_NOPE_DIM + V_HEAD_DIM)
    return kv[..., :QK_NOPE_DIM], kv[..., QK_NOPE_DIM:]


def mla_attn_context(h, w_in, q_norm, w_q_b, kv_norm, w_kv_b, w_o):
    B, L, _ = h.shape
    q_nope, q_pe, c_kv, k_pe = mla_project(h, w_in, q_norm, w_q_b, kv_norm)
    k_nope, v = mla_expand(c_kv, w_kv_b)
    logits = (jnp.einsum('bqhd,bshd->bhqs', q_nope, k_nope)
              + jnp.einsum('bqhr,bsr->bhqs', q_pe, k_pe)).astype(jnp.float32) * MLA_SCALE
    p = jax.nn.softmax(logits, axis=-1).astype(v.dtype)
    o = jnp.einsum('bhqs,bshd->bqhd', p, v).reshape(B, L, MLA_HEADS * V_HEAD_DIM)
    return o @ w_o, c_kv, k_pe


def mla_attn_latent(h, ctx_ckv, ctx_kpe, rows, cols, w_in, q_norm, w_q_b, kv_norm, w_kv_b, w_o):
    B, L, _ = h.shape
    nb = L // BLOCK
    q_nope, q_pe, c_kv, k_pe = mla_project(h, w_in, q_norm, w_q_b, kv_norm)
    q_pe = rope_2d(q_pe, rows, cols)
    k_pe = rope_2d(k_pe[:, :, None, :], rows, cols)[:, :, 0, :]
    k_nope, v = mla_expand(jnp.concatenate([c_kv, ctx_ckv], axis=1), w_kv_b)
    k_pe = jnp.concatenate([k_pe, ctx_kpe], axis=1)
    qn_b = q_nope.reshape(B, nb, BLOCK, MLA_HEADS, QK_NOPE_DIM).transpose(1, 0, 2, 3, 4)
    qp_b = q_pe.reshape(B, nb, BLOCK, MLA_HEADS, QK_ROPE_DIM).transpose(1, 0, 2, 3, 4)

    def attend_block(qb):
        qn, qp = qb
        logits = (jnp.einsum('bqhd,bshd->bhqs', qn, k_nope)
                  + jnp.einsum('bqhr,bsr->bhqs', qp, k_pe)).astype(jnp.float32) * MLA_SCALE
        p = jax.nn.softmax(logits, axis=-1).astype(v.dtype)
        return jnp.einsum('bhqs,bshd->bqhd', p, v)

    o = lax.map(attend_block, (qn_b, qp_b))
    o = o.transpose(1, 0, 2, 3, 4).reshape(B, L, MLA_HEADS * V_HEAD_DIM)
    return o @ w_o


def swiglu(h, w_gate, w_up, w_down):
    return (jax.nn.silu(h @ w_gate) * (h @ w_up)) @ w_down


def setup_inputs(seed: int = 0) -> dict:
    key = jax.random.key(seed)
    ks = iter(jax.random.split(key, 32))

    def nrm(shape, scale):
        return jax.random.normal(next(ks), shape, jnp.float32) * scale

    def gain(shape):
        return 1.0 + nrm(shape, 0.1)

    qkv_w = (WIN_HEADS + 2 * WIN_KV_HEADS) * WIN_HEAD_DIM
    return {
        "x_prompt": nrm((BATCH, SEQ, D_MODEL), 1.0),
        "x_sample": nrm((DEC_BATCH, DEC_SEQ, D_MODEL), 1.0),
        "cache_win_k": nrm((DEC_BATCH, N_WIN_LAYERS, PAST_LEN, WIN_KV_HEADS, WIN_HEAD_DIM), 1.0),
        "cache_win_v": nrm((DEC_BATCH, N_WIN_LAYERS, PAST_LEN, WIN_KV_HEADS, WIN_HEAD_DIM), 1.0),
        "cache_mla_ckv": nrm((DEC_BATCH, N_MLA_LAYERS, PAST_LEN, KV_LORA_RANK), 1.0),
        "cache_mla_kpe": nrm((DEC_BATCH, N_MLA_LAYERS, PAST_LEN, QK_ROPE_DIM), 1.0),
        "c": nrm((DEC_BATCH, D_MODEL), 1.0),
        "c_ctx": nrm((D_MODEL,), 1.0),
        "ada_w": nrm((DEPTH, D_MODEL, 6 * D_MODEL), 0.5 * D_MODEL ** -0.5),
        "ada_b": nrm((DEPTH, 6 * D_MODEL), 0.02),
        "norm_mix": gain((DEPTH, D_MODEL)),
        "norm_ffn": gain((DEPTH, D_MODEL)),
        "win_w_qkv": nrm((N_WIN_LAYERS, D_MODEL, qkv_w), D_MODEL ** -0.5),
        "win_w_o": nrm((N_WIN_LAYERS, WIN_HEADS * WIN_HEAD_DIM, D_MODEL), (WIN_HEADS * WIN_HEAD_DIM) ** -0.5),
        "win_sink": nrm((N_WIN_LAYERS, WIN_HEADS), 0.5),
        "mla_w_in": nrm((N_MLA_LAYERS, D_MODEL, Q_LORA_RANK + KV_LORA_RANK + QK_ROPE_DIM), D_MODEL ** -0.5),
        "mla_q_norm": gain((N_MLA_LAYERS, Q_LORA_RANK)),
        "mla_w_q_b": nrm((N_MLA_LAYERS, Q_LORA_RANK, MLA_HEADS * (QK_NOPE_DIM + QK_ROPE_DIM)), Q_LORA_RANK ** -0.5),
        "mla_kv_norm": gain((N_MLA_LAYERS, KV_LORA_RANK)),
        "mla_w_kv_b": nrm((N_MLA_LAYERS, KV_LORA_RANK, MLA_HEADS * (QK_NOPE_DIM + V_HEAD_DIM)), KV_LORA_RANK ** -0.5),
        "mla_w_o": nrm((N_MLA_LAYERS, MLA_HEADS * V_HEAD_DIM, D_MODEL), (MLA_HEADS * V_HEAD_DIM) ** -0.5),
        "ffn_w_gate": nrm((DEPTH, D_MODEL, D_FF), D_MODEL ** -0.5),
        "ffn_w_up": nrm((DEPTH, D_MODEL, D_FF), D_MODEL ** -0.5),
        "ffn_w_down": nrm((DEPTH, D_FF, D_MODEL), D_FF ** -0.5),
        "norm_final": gain((D_MODEL,)),
    }


def reference(x_prompt, x_sample, cache_win_k, cache_win_v, cache_mla_ckv, cache_mla_kpe, c,
              c_ctx, ada_w, ada_b, norm_mix, norm_ffn, win_w_qkv, win_w_o, win_sink,
              mla_w_in, mla_q_norm, mla_w_q_b, mla_kv_norm, mla_w_kv_b, mla_w_o,
              ffn_w_gate, ffn_w_up, ffn_w_down, norm_final):
    rows, cols = grid_positions(x_sample.shape[1])
    xc, xs = x_prompt, x_sample
    win_k_out, win_v_out, ckv_out, kpe_out = [], [], [], []
    for layer in range(DEPTH):
        j = layer // N_MIXERS
        sh_mc, sc_mc, g_mc, sh_fc, sc_fc, g_fc = ada_modulation(c_ctx[None, :], ada_w[layer], ada_b[layer])
        sh_ms, sc_ms, g_ms, sh_fs, sc_fs, g_fs = ada_modulation(c, ada_w[layer], ada_b[layer])
        hc = modulate(xc, norm_mix[layer], sh_mc, sc_mc)
        hs = modulate(xs, norm_mix[layer], sh_ms, sc_ms)
        if layer % N_MIXERS == 0:
            oc, k_ctx, v_ctx = win_attn_context(hc, win_w_qkv[j], win_w_o[j], win_sink[j])
            os_ = win_attn_latent(hs, cache_win_k[:, j], cache_win_v[:, j], rows, cols,
                                  win_w_qkv[j], win_w_o[j], win_sink[j])
            win_k_out.append(k_ctx)
            win_v_out.append(v_ctx)
        else:
            oc, ckv_ctx, kpe_ctx = mla_attn_context(hc, mla_w_in[j], mla_q_norm[j], mla_w_q_b[j],
                                                    mla_kv_norm[j], mla_w_kv_b[j], mla_w_o[j])
            os_ = mla_attn_latent(hs, cache_mla_ckv[:, j], cache_mla_kpe[:, j], rows, cols,
                                  mla_w_in[j], mla_q_norm[j], mla_w_q_b[j],
                                  mla_kv_norm[j], mla_w_kv_b[j], mla_w_o[j])
            ckv_out.append(ckv_ctx)
            kpe_out.append(kpe_ctx)
        xc = xc + g_mc * oc
        xs = xs + g_ms * os_
        hc = modulate(xc, norm_ffn[layer], sh_fc, sc_fc)
        hs = modulate(xs, norm_ffn[layer], sh_fs, sc_fs)
        xc = xc + g_fc * swiglu(hc, ffn_w_gate[layer], ffn_w_up[layer], ffn_w_down[layer])
        xs = xs + g_fs * swiglu(hs, ffn_w_gate[layer], ffn_w_up[layer], ffn_w_down[layer])
    y_prompt = rms_norm(xc, norm_final)
    y_sample = rms_norm(xs, norm_final)
    new_win_k = jnp.stack(win_k_out, axis=1)
    new_win_v = jnp.stack(win_v_out, axis=1)
    new_mla_ckv = jnp.stack(ckv_out, axis=1)
    new_mla_kpe = jnp.stack(kpe_out, axis=1)
    return (y_prompt, y_sample, new_win_k, new_win_v, new_mla_ckv, new_mla_kpe)
```

```python
import functools

import numpy as np
import jax
import jax.numpy as jnp
from jax import lax
from jax.experimental import pallas as pl
from jax.experimental.pallas import tpu as pltpu

F32 = jnp.float32
BF16 = jnp.bfloat16

D_MODEL = 2048
BATCH = 16
SEQ = 256
DEPTH = 4
DEC_BATCH = 2
DEC_SEQ = 2048
PAST_LEN = 256
GRID_W = 64
N_WIN_LAYERS = 2
N_MLA_LAYERS = 2
WIN_HEADS = 32
WIN_KV_HEADS = 8
WIN_HEAD_DIM = 64
WINDOW = 128
BLOCK = 128
WIN_SCALE = WIN_HEAD_DIM ** -0.5
MLA_HEADS = 16
Q_LORA_RANK = 512
KV_LORA_RANK = 512
QK_NOPE_DIM = 128
QK_ROPE_DIM = 64
V_HEAD_DIM = 128
MLA_SCALE = (QK_NOPE_DIM + QK_ROPE_DIM) ** -0.5
D_FF = 5632
ROPE_BASE = 10000.0
EPS = 1e-6
NEG = float(np.finfo(np.float32).min)

N_CTX = BATCH * SEQ
N_LAT = DEC_BATCH * DEC_SEQ
M_ALL = N_CTX + N_LAT
MOD_ROWS = 16
LANES = 128
HALF_LANES = LANES // 2
WIN_QD = WIN_HEADS * WIN_HEAD_DIM
WIN_KD = WIN_KV_HEADS * WIN_HEAD_DIM
MLA_IN_EXT = Q_LORA_RANK + KV_LORA_RANK + LANES
VMEM_LIMIT = 56 * 1024 * 1024


def _params(n_axes, vmem=VMEM_LIMIT):
    return pltpu.CompilerParams(dimension_semantics=("arbitrary",) * n_axes,
                                vmem_limit_bytes=vmem)


def _group(i, tm):
    start = i * tm
    return jnp.where(start < N_CTX, 0, 1 + (start - N_CTX) // DEC_SEQ)


def _mod_spec(layer, chunk, tm, tn, m_axis, n_axis):
    nb = D_MODEL // tn

    def imap(*ids):
        j = 0 if n_axis is None else ids[n_axis]
        return (layer * MOD_ROWS + _group(ids[m_axis], tm), 0, chunk * nb + j)

    return pl.BlockSpec((None, 1, tn), imap)


def _rms(x, g):
    y = x * lax.rsqrt(jnp.mean(x * x, axis=-1, keepdims=True) + EPS)
    return y * g


def _dot(a, b):
    return jnp.dot(a, b, preferred_element_type=F32)


def _dot_nt(a, b):
    return lax.dot_general(a, b, (((1,), (1,)), ((), ())), preferred_element_type=F32)


def _rope(x, c, s):
    lane = lax.broadcasted_iota(jnp.int32, (x.shape[0], LANES), 1)
    first = (lane & 31) < 16
    outs = []
    for k in range(x.shape[1] // LANES):
        a = x[:, k * LANES:(k + 1) * LANES]
        partner = jnp.where(first, pltpu.roll(a, LANES - 16, 1), pltpu.roll(a, 16, 1))
        outs.append(a * c + partner * s)
    return outs[0] if len(outs) == 1 else jnp.concatenate(outs, axis=1)


def _ada_body(cond_ref, w_ref, b_ref, o_ref):
    s = jax.nn.silu(cond_ref[...])
    o_ref[...] = _dot(s.astype(BF16), w_ref[...].astype(BF16)) + b_ref[...]


def _ada_call(cond, ada_w, ada_b):
    tn = 1024
    n = 6 * D_MODEL
    return pl.pallas_call(
        _ada_body,
        grid=(DEPTH, n // tn),
        in_specs=[
            pl.BlockSpec((MOD_ROWS, D_MODEL), lambda l, j: (0, 0)),
            pl.BlockSpec((None, D_MODEL, tn), lambda l, j: (l, 0, j)),
            pl.BlockSpec((None, 1, tn), lambda l, j: (l, 0, j)),
        ],
        out_specs=pl.BlockSpec((None, MOD_ROWS, tn), lambda l, j: (l, 0, j)),
        out_shape=jax.ShapeDtypeStruct((DEPTH, MOD_ROWS, n), F32),
        compiler_params=_params(2),
        name="ada_modulation",
    )(cond, ada_w, ada_b.reshape(DEPTH, 1, n))


def _modulate_body(x_ref, g_ref, sc_ref, sh_ref, o_ref):
    y = _rms(x_ref[...], g_ref[...])
    o_ref[...] = (y * (1.0 + sc_ref[...]) + sh_ref[...]).astype(o_ref.dtype)


def _modulate_call(x, gains, layer, mod, shift_chunk, scale_chunk):
    tm = 512
    return pl.pallas_call(
        _modulate_body,
        grid=(M_ALL // tm,),
        in_specs=[
            pl.BlockSpec((tm, D_MODEL), lambda i: (i, 0)),
            pl.BlockSpec((None, 1, D_MODEL), lambda i: (layer, 0, 0)),
            _mod_spec(layer, scale_chunk, tm, D_MODEL, 0, None),
            _mod_spec(layer, shift_chunk, tm, D_MODEL, 0, None),
        ],
        out_specs=pl.BlockSpec((tm, D_MODEL), lambda i: (i, 0)),
        out_shape=jax.ShapeDtypeStruct((M_ALL, D_MODEL), BF16),
        compiler_params=_params(1),
        name="norm_modulate",
    )(x, gains.reshape(DEPTH, 1, D_MODEL), mod, mod)


def _final_norm_body(x_ref, g_ref, o_ref):
    o_ref[...] = _rms(x_ref[...], g_ref[...])


def _final_norm_call(x, gain, row_off, n_rows):
    tm = 512
    off = row_off // tm
    return pl.pallas_call(
        _final_norm_body,
        grid=(n_rows // tm,),
        in_specs=[
            pl.BlockSpec((tm, D_MODEL), lambda i: (off + i, 0)),
            pl.BlockSpec((1, D_MODEL), lambda i: (0, 0)),
        ],
        out_specs=pl.BlockSpec((tm, D_MODEL), lambda i: (i, 0)),
        out_shape=jax.ShapeDtypeStruct((n_rows, D_MODEL), F32),
        compiler_params=_params(1),
        name="final_norm",
    )(x, gain.reshape(1, D_MODEL))


def _proj_body(*refs, rope_lo, rope_hi, n_tiles):
    if rope_hi > rope_lo:
        lhs_ref, w_ref, cos_ref, sin_ref, o_ref = refs
    else:
        lhs_ref, w_ref, o_ref = refs
    acc = _dot(lhs_ref[...].astype(BF16), w_ref[...].astype(BF16))
    if rope_hi <= rope_lo:
        o_ref[...] = acc.astype(o_ref.dtype)
    elif rope_lo == 0 and rope_hi == n_tiles:
        o_ref[...] = _rope(acc, cos_ref[...], sin_ref[...]).astype(o_ref.dtype)
    else:
        j = pl.program_id(1)
        roped = jnp.logical_and(j >= rope_lo, j < rope_hi)

        @pl.when(roped)
        def _():
            o_ref[...] = _rope(acc, cos_ref[...], sin_ref[...]).astype(o_ref.dtype)

        @pl.when(jnp.logical_not(roped))
        def _():
            o_ref[...] = acc.astype(o_ref.dtype)


def _proj_call(lhs, w, layer, col_off, n_cols, out_dtype, *, tm, tn, rope=None, tables=None, name):
    m, k = lhs.shape
    n_tiles = n_cols // tn
    off = col_off // tn
    rope_lo, rope_hi = rope if rope is not None else (0, 0)
    if w.ndim == 3:
        w_spec = pl.BlockSpec((None, k, tn), lambda i, j: (layer, 0, off + j))
    else:
        w_spec = pl.BlockSpec((k, tn), lambda i, j: (0, off + j))
    in_specs = [pl.BlockSpec((tm, k), lambda i, j: (i, 0)), w_spec]
    args = [lhs, w]
    if rope_hi > rope_lo:
        in_specs += [pl.BlockSpec((tm, LANES), lambda i, j: (i, 0))] * 2
        args += list(tables)
    return pl.pallas_call(
        functools.partial(_proj_body, rope_lo=rope_lo, rope_hi=rope_hi, n_tiles=n_tiles),
        grid=(m // tm, n_tiles),
        in_specs=in_specs,
        out_specs=pl.BlockSpec((tm, tn), lambda i, j: (i, j)),
        out_shape=jax.ShapeDtypeStruct((m, n_cols), out_dtype),
        compiler_params=_params(2),
        name=name,
    )(*args)


def _resid_body(lhs_ref, w_ref, x_ref, gate_ref, o_ref):
    acc = _dot(lhs_ref[...], w_ref[...].astype(BF16))
    o_ref[...] = x_ref[...] + gate_ref[...] * acc


def _resid_call(lhs, w, layer, x, mod, mod_layer, gate_chunk, *, tm, tn, name):
    m, k = lhs.shape
    return pl.pallas_call(
        _resid_body,
        grid=(m // tm, D_MODEL // tn),
        in_specs=[
            pl.BlockSpec((tm, k), lambda i, j: (i, 0)),
            pl.BlockSpec((None, k, tn), lambda i, j: (layer, 0, j)),
            pl.BlockSpec((tm, tn), lambda i, j: (i, j)),
            _mod_spec(mod_layer, gate_chunk, tm, tn, 0, 1),
        ],
        out_specs=pl.BlockSpec((tm, tn), lambda i, j: (i, j)),
        out_shape=jax.ShapeDtypeStruct((m, D_MODEL), F32),
        compiler_params=_params(2),
        name=name,
    )(lhs, w, x, mod)


def _swiglu_body(h_ref, wg_ref, wu_ref, o_ref, wg_s, wu_s):
    @pl.when(pl.program_id(1) == 0)
    def _():
        wg_s[...] = wg_ref[...].astype(BF16)
        wu_s[...] = wu_ref[...].astype(BF16)

    h = h_ref[...]
    g = _dot(h, wg_s[...])
    u = _dot(h, wu_s[...])
    o_ref[...] = (jax.nn.silu(g) * u).astype(o_ref.dtype)


def _swiglu_call(h, w_gate, w_up, layer):
    tm, tn = 1024, 512
    return pl.pallas_call(
        _swiglu_body,
        grid=(D_FF // tn, M_ALL // tm),
        in_specs=[
            pl.BlockSpec((tm, D_MODEL), lambda j, i: (i, 0)),
            pl.BlockSpec((None, D_MODEL, tn), lambda j, i: (layer, 0, j)),
            pl.BlockSpec((None, D_MODEL, tn), lambda j, i: (layer, 0, j)),
        ],
        out_specs=pl.BlockSpec((tm, tn), lambda j, i: (i, j)),
        out_shape=jax.ShapeDtypeStruct((M_ALL, D_FF), BF16),
        scratch_shapes=[pltpu.VMEM((D_MODEL, tn), BF16), pltpu.VMEM((D_MODEL, tn), BF16)],
        compiler_params=_params(2),
        name="ffn_gate_up",
    )(h, w_gate, w_up)


def _mla_in_body(h_ref, w_ref, qn_ref, kvn_ref, cos_ref, sin_ref, cq_ref, ckv_ref, kpe_ref):
    acc = _dot(h_ref[...], w_ref[...].astype(BF16))
    cq = acc[:, :Q_LORA_RANK]
    ckv = acc[:, Q_LORA_RANK:Q_LORA_RANK + KV_LORA_RANK]
    kpe = acc[:, Q_LORA_RANK + KV_LORA_RANK:]
    cq_ref[...] = _rms(cq, qn_ref[...]).astype(cq_ref.dtype)
    ckv_ref[...] = _rms(ckv, kvn_ref[...])
    kpe_ref[...] = _rope(kpe, cos_ref[...], sin_ref[...])


def _mla_in_call(h, w_ext, q_norm, kv_norm, tables):
    tm = 1024
    row = lambda i: (i, 0)
    fixed = lambda i: (0, 0)
    return pl.pallas_call(
        _mla_in_body,
        grid=(M_ALL // tm,),
        in_specs=[
            pl.BlockSpec((tm, D_MODEL), row),
            pl.BlockSpec((D_MODEL, MLA_IN_EXT), fixed),
            pl.BlockSpec((1, Q_LORA_RANK), fixed),
            pl.BlockSpec((1, KV_LORA_RANK), fixed),
            pl.BlockSpec((tm, LANES), row),
            pl.BlockSpec((tm, LANES), row),
        ],
        out_specs=[
            pl.BlockSpec((tm, Q_LORA_RANK), row),
            pl.BlockSpec((tm, KV_LORA_RANK), row),
            pl.BlockSpec((tm, LANES), row),
        ],
        out_shape=[
            jax.ShapeDtypeStruct((M_ALL, Q_LORA_RANK), BF16),
            jax.ShapeDtypeStruct((M_ALL, KV_LORA_RANK), F32),
            jax.ShapeDtypeStruct((M_ALL, LANES), F32),
        ],
        compiler_params=_params(1),
        name="mla_in_proj",
    )(h, w_ext, q_norm.reshape(1, -1), kv_norm.reshape(1, -1), *tables)


def _half_masks(rows):
    lane = lax.broadcasted_iota(jnp.int32, (rows, LANES), 1)
    low = lane < HALF_LANES
    return low, low.astype(F32).astype(BF16), jnp.logical_not(low).astype(F32).astype(BF16)


def _both_halves(chunk, head_in_low, low):
    rolled = pltpu.roll(chunk, HALF_LANES, 1)
    return jnp.where(low, chunk, rolled) if head_in_low else jnp.where(low, rolled, chunk)


def _win_heads(q_ref, o_ref, sink_ref, layer, k_chunks, v_chunks, mask, nq):
    nk = k_chunks[0].shape[0]
    low_k, _, _ = _half_masks(nk)
    _, qlo, qhi = _half_masks(nq)
    _, vlo, vhi = _half_masks(nk)
    for kh in range(WIN_KV_HEADS):
        in_low = kh % 2 == 0
        k2 = _both_halves(k_chunks[kh // 2], in_low, low_k).astype(BF16)
        v2 = _both_halves(v_chunks[kh // 2], in_low, low_k).astype(BF16)
        v2lo, v2hi = v2 * vlo, v2 * vhi
        for c in (2 * kh, 2 * kh + 1):
            qc = q_ref[:, c * LANES:(c + 1) * LANES]
            qs = jnp.concatenate([qc * qlo, qc * qhi], axis=0)
            lg = _dot_nt(qs, k2) * WIN_SCALE
            if mask is not None:
                lg = jnp.where(mask, lg, NEG)
            probs = []
            for t in range(2):
                sk = sink_ref[layer, 2 * c + t]
                l = lg[t * nq:(t + 1) * nq]
                m = jnp.maximum(jnp.max(l, axis=-1, keepdims=True), sk)
                e = jnp.exp(l - m)
                den = jnp.sum(e, axis=-1, keepdims=True) + jnp.exp(sk - m)
                probs.append((e * (1.0 / den)).astype(BF16))
            o = _dot(probs[0], v2lo) + _dot(probs[1], v2hi)
            o_ref[:, c * LANES:(c + 1) * LANES] = o.astype(o_ref.dtype)


def _win_ctx_body(sink_ref, q_ref, kv_ref, o_ref, *, layer):
    n_pairs = WIN_KD // LANES
    k_chunks = [kv_ref[:, p * LANES:(p + 1) * LANES] for p in range(n_pairs)]
    v_chunks = [kv_ref[:, WIN_KD + p * LANES:WIN_KD + (p + 1) * LANES] for p in range(n_pairs)]
    _win_heads(q_ref, o_ref, sink_ref, layer, k_chunks, v_chunks, None, SEQ)


def _win_ctx_call(q, kv, sink, layer):
    return pl.pallas_call(
        functools.partial(_win_ctx_body, layer=layer),
        grid=(BATCH,),
        in_specs=[
            pl.BlockSpec(memory_space=pltpu.SMEM),
            pl.BlockSpec((SEQ, WIN_QD), lambda b: (b, 0)),
            pl.BlockSpec((SEQ, 2 * WIN_KD), lambda b: (b, 0)),
        ],
        out_specs=pl.BlockSpec((SEQ, WIN_QD), lambda b: (b, 0)),
        out_shape=jax.ShapeDtypeStruct((N_CTX, WIN_QD), BF16),
        compiler_params=_params(1),
        name="win_attn_context",
    )(sink, q, kv)


def _win_lat_body(sink_ref, q_ref, kvp_ref, kvc_ref, kvn_ref, ck_ref, cv_ref, o_ref, *, layer):
    n = pl.program_id(1)
    n_pairs = WIN_KD // LANES
    k_chunks, v_chunks = [], []
    for p in range(n_pairs):
        ks = slice(p * LANES, (p + 1) * LANES)
        vs = slice(WIN_KD + p * LANES, WIN_KD + (p + 1) * LANES)
        k_chunks.append(jnp.concatenate([kvp_ref[:, ks], kvc_ref[:, ks], kvn_ref[:, ks], ck_ref[:, ks]], axis=0))
        v_chunks.append(jnp.concatenate([kvp_ref[:, vs], kvc_ref[:, vs], kvn_ref[:, vs], cv_ref[:, ks]], axis=0))
    nk = 3 * BLOCK + PAST_LEN
    r = lax.broadcasted_iota(jnp.int32, (2 * BLOCK, nk), 0) & (BLOCK - 1)
    s = lax.broadcasted_iota(jnp.int32, (2 * BLOCK, nk), 1)
    key_pos = n * BLOCK - WINDOW + s
    dist = s - WINDOW - r
    in_window = (dist <= WINDOW) & (dist >= -WINDOW) & (key_pos >= 0) & (key_pos < DEC_SEQ)
    mask = (s >= 3 * BLOCK) | in_window
    _win_heads(q_ref, o_ref, sink_ref, layer, k_chunks, v_chunks, mask, BLOCK)


def _win_lat_call(q, kv, cache_k, cache_v, sink, layer):
    nb = DEC_SEQ // BLOCK
    base = N_CTX // BLOCK

    def rows(shift):
        return lambda b, n: (base + b * nb + jnp.clip(n + shift, 0, nb - 1), 0)

    cache_spec = pl.BlockSpec((None, None, PAST_LEN, WIN_KD), lambda b, n: (b, layer, 0, 0))
    return pl.pallas_call(
        functools.partial(_win_lat_body, layer=layer),
        grid=(DEC_BATCH, nb),
        in_specs=[
            pl.BlockSpec(memory_space=pltpu.SMEM),
            pl.BlockSpec((BLOCK, WIN_QD), rows(0)),
            pl.BlockSpec((BLOCK, 2 * WIN_KD), rows(-1)),
            pl.BlockSpec((BLOCK, 2 * WIN_KD), rows(0)),
            pl.BlockSpec((BLOCK, 2 * WIN_KD), rows(1)),
            cache_spec,
            cache_spec,
        ],
        out_specs=pl.BlockSpec((BLOCK, WIN_QD), lambda b, n: (b * nb + n, 0)),
        out_shape=jax.ShapeDtypeStruct((N_LAT, WIN_QD), BF16),
        compiler_params=_params(2),
        name="win_attn_latent",
    )(sink, q, kv, kv, kv, cache_k, cache_v)


KV_HEAD_W = QK_NOPE_DIM + V_HEAD_DIM
MLA_NOPE_W = MLA_HEADS * QK_NOPE_DIM
MLA_ROPE_W = MLA_HEADS * QK_ROPE_DIM


def _mla_ctx_body(qn_ref, qp_ref, kvx_ref, kpe_ref, o_ref):
    _, qlo, qhi = _half_masks(SEQ)
    kpe2 = kpe_ref[...].astype(BF16)
    for h in range(MLA_HEADS):
        qp = qp_ref[:, (h // 2) * LANES:(h // 2 + 1) * LANES] * (qlo if h % 2 == 0 else qhi)
        qcat = jnp.concatenate([qn_ref[:, h * LANES:(h + 1) * LANES], qp], axis=1)
        kcat = jnp.concatenate([kvx_ref[:, h * KV_HEAD_W:h * KV_HEAD_W + QK_NOPE_DIM], kpe2], axis=1)
        lg = _dot_nt(qcat, kcat) * MLA_SCALE
        m = jnp.max(lg, axis=-1, keepdims=True)
        e = jnp.exp(lg - m)
        p = (e * (1.0 / jnp.sum(e, axis=-1, keepdims=True))).astype(BF16)
        o = _dot(p, kvx_ref[:, h * KV_HEAD_W + QK_NOPE_DIM:(h + 1) * KV_HEAD_W])
        o_ref[:, h * LANES:(h + 1) * LANES] = o.astype(o_ref.dtype)


def _mla_ctx_call(q, kvx, kpe):
    return pl.pallas_call(
        _mla_ctx_body,
        grid=(BATCH,),
        in_specs=[
            pl.BlockSpec((SEQ, MLA_NOPE_W), lambda b: (b, 0)),
            pl.BlockSpec((SEQ, MLA_ROPE_W), lambda b: (b, MLA_NOPE_W // MLA_ROPE_W)),
            pl.BlockSpec((SEQ, MLA_HEADS * KV_HEAD_W), lambda b: (b, 0)),
            pl.BlockSpec((SEQ, LANES), lambda b: (b, 0)),
        ],
        out_specs=pl.BlockSpec((SEQ, MLA_HEADS * V_HEAD_DIM), lambda b: (b, 0)),
        out_shape=jax.ShapeDtypeStruct((N_CTX, MLA_HEADS * V_HEAD_DIM), BF16),
        compiler_params=_params(1),
        name="mla_attn_context",
    )(q, q, kvx, kpe)


MLA_QB = 512


def _mla_lat_body(qn_ref, qp_ref, kvl_ref, kvc_ref, kpl_ref, kpc_ref, o_ref):
    _, qlo, qhi = _half_masks(MLA_QB)
    kpl = kpl_ref[...].astype(BF16)
    kpc = kpc_ref[...].astype(BF16)
    qp_pair = qp_ref[...]
    for t in range(2):
        c0 = t * KV_HEAD_W
        qcat = jnp.concatenate([qn_ref[:, t * LANES:(t + 1) * LANES], qp_pair * (qlo if t == 0 else qhi)], axis=1)
        k_lat = jnp.concatenate([kvl_ref[:, c0:c0 + QK_NOPE_DIM], kpl], axis=1)
        k_cache = jnp.concatenate([kvc_ref[:, c0:c0 + QK_NOPE_DIM], kpc], axis=1)
        l1 = _dot_nt(qcat, k_lat) * MLA_SCALE
        l2 = _dot_nt(qcat, k_cache) * MLA_SCALE
        m = jnp.maximum(jnp.max(l1, axis=-1, keepdims=True), jnp.max(l2, axis=-1, keepdims=True))
        e1 = jnp.exp(l1 - m)
        e2 = jnp.exp(l2 - m)
        inv = 1.0 / (jnp.sum(e1, axis=-1, keepdims=True) + jnp.sum(e2, axis=-1, keepdims=True))
        o = (_dot((e1 * inv).astype(BF16), kvl_ref[:, c0 + QK_NOPE_DIM:c0 + KV_HEAD_W])
             + _dot((e2 * inv).astype(BF16), kvc_ref[:, c0 + QK_NOPE_DIM:c0 + KV_HEAD_W]))
        o_ref[:, t * LANES:(t + 1) * LANES] = o.astype(o_ref.dtype)


def _mla_lat_call(q, kvx, kvx_cache, kpe, kpe_cache):
    n_qb = DEC_SEQ // MLA_QB
    n_pairs = MLA_HEADS // 2
    lat_blk = N_CTX // DEC_SEQ
    return pl.pallas_call(
        _mla_lat_body,
        grid=(DEC_BATCH, n_pairs, n_qb),
        in_specs=[
            pl.BlockSpec((MLA_QB, 2 * QK_NOPE_DIM), lambda b, hp, qb: (N_CTX // MLA_QB + b * n_qb + qb, hp)),
            pl.BlockSpec((MLA_QB, LANES), lambda b, hp, qb: (N_CTX // MLA_QB + b * n_qb + qb, MLA_NOPE_W // LANES + hp)),
            pl.BlockSpec((DEC_SEQ, 2 * KV_HEAD_W), lambda b, hp, qb: (lat_blk + b, hp)),
            pl.BlockSpec((PAST_LEN, 2 * KV_HEAD_W), lambda b, hp, qb: (b, hp)),
            pl.BlockSpec((DEC_SEQ, LANES), lambda b, hp, qb: (lat_blk + b, 0)),
            pl.BlockSpec((None, PAST_LEN, LANES), lambda b, hp, qb: (b, 0, 0)),
        ],
        out_specs=pl.BlockSpec((MLA_QB, 2 * V_HEAD_DIM), lambda b, hp, qb: (b * n_qb + qb, hp)),
        out_shape=jax.ShapeDtypeStruct((N_LAT, MLA_HEADS * V_HEAD_DIM), BF16),
        compiler_params=_params(3),
        name="mla_attn_latent",
    )(q, q, kvx, kvx_cache, kpe, kpe_cache)


def _rope_tables():
    t = jnp.arange(DEC_SEQ, dtype=jnp.int32)
    rows, cols = t // GRID_W, t % GRID_W
    half = WIN_HEAD_DIM // 4
    freqs = ROPE_BASE ** (-jnp.arange(half, dtype=F32) / half)
    ang_r = rows.astype(F32)[:, None] * freqs[None, :]
    ang_c = cols.astype(F32)[:, None] * freqs[None, :]
    cr, sr, cc, sc = jnp.cos(ang_r), jnp.sin(ang_r), jnp.cos(ang_c), jnp.sin(ang_c)
    c64 = jnp.concatenate([cr, cr, cc, cc], axis=-1)
    s64 = jnp.concatenate([-sr, sr, -sc, sc], axis=-1)
    c_lat = jnp.tile(c64, (DEC_BATCH, LANES // WIN_HEAD_DIM))
    s_lat = jnp.tile(s64, (DEC_BATCH, LANES // WIN_HEAD_DIM))
    cos_t = jnp.concatenate([jnp.ones((N_CTX, LANES), F32), c_lat], axis=0)
    sin_t = jnp.concatenate([jnp.zeros((N_CTX, LANES), F32), s_lat], axis=0)
    return cos_t, sin_t


def kernel(x_prompt, x_sample, cache_win_k, cache_win_v, cache_mla_ckv, cache_mla_kpe, c, c_ctx, ada_w, ada_b, norm_mix, norm_ffn, win_w_qkv, win_w_o, win_sink, mla_w_in, mla_q_norm, mla_w_q_b, mla_kv_norm, mla_w_kv_b, mla_w_o, ffn_w_gate, ffn_w_up, ffn_w_down, norm_final):
    x = jnp.concatenate([x_prompt.reshape(N_CTX, D_MODEL), x_sample.reshape(N_LAT, D_MODEL)], axis=0)
    cond = jnp.concatenate([c_ctx[None, :], c, jnp.zeros((MOD_ROWS - 1 - DEC_BATCH, D_MODEL), F32)], axis=0)
    mod = _ada_call(cond, ada_w, ada_b).reshape(DEPTH * MOD_ROWS, 1, 6 * D_MODEL)
    tables = _rope_tables()
    cache_k = cache_win_k.reshape(DEC_BATCH, N_WIN_LAYERS, PAST_LEN, WIN_KD)
    cache_v = cache_win_v.reshape(DEC_BATCH, N_WIN_LAYERS, PAST_LEN, WIN_KD)

    win_k_out, win_v_out, ckv_out, kpe_out = [], [], [], []
    for layer in range(DEPTH):
        j = layer // 2
        h = _modulate_call(x, norm_mix, layer, mod, 0, 1)
        if layer % 2 == 0:
            q = _proj_call(h, win_w_qkv, j, 0, WIN_QD, BF16, tm=1024, tn=512,
                           rope=(0, WIN_QD // 512), tables=tables, name="win_q_proj")
            kv = _proj_call(h, win_w_qkv, j, WIN_QD, 2 * WIN_KD, F32, tm=1024, tn=512,
                            rope=(0, 1), tables=tables, name="win_kv_proj")
            o_ctx = _win_ctx_call(q, kv, win_sink, j)
            o_lat = _win_lat_call(q, kv, cache_k, cache_v, win_sink, j)
            w_o = win_w_o
            win_k_out.append(kv[:N_CTX, :WIN_KD].reshape(BATCH, SEQ, WIN_KV_HEADS, WIN_HEAD_DIM))
            win_v_out.append(kv[:N_CTX, WIN_KD:].reshape(BATCH, SEQ, WIN_KV_HEADS, WIN_HEAD_DIM))
        else:
            w_in = mla_w_in[j]
            w_ext = jnp.concatenate([w_in, w_in[:, Q_LORA_RANK + KV_LORA_RANK:]], axis=1)
            cq, ckv, kpe = _mla_in_call(h, w_ext, mla_q_norm[j], mla_kv_norm[j], tables)
            wq = mla_w_q_b[j].reshape(Q_LORA_RANK, MLA_HEADS, QK_NOPE_DIM + QK_ROPE_DIM)
            wq = jnp.concatenate([wq[:, :, :QK_NOPE_DIM].reshape(Q_LORA_RANK, MLA_NOPE_W),
                                  wq[:, :, QK_NOPE_DIM:].reshape(Q_LORA_RANK, MLA_ROPE_W)], axis=1)
            q = _proj_call(cq, wq, 0, 0, MLA_NOPE_W + MLA_ROPE_W, BF16, tm=1024, tn=512,
                           rope=(MLA_NOPE_W // 512, (MLA_NOPE_W + MLA_ROPE_W) // 512), tables=tables,
                           name="mla_q_proj")
            kvx = _proj_call(ckv, mla_w_kv_b, j, 0, MLA_HEADS * KV_HEAD_W, BF16, tm=1024, tn=1024,
                             name="mla_kv_expand")
            cache_ckv = cache_mla_ckv[:, j].reshape(DEC_BATCH * PAST_LEN, KV_LORA_RANK)
            kvx_cache = _proj_call(cache_ckv, mla_w_kv_b, j, 0, MLA_HEADS * KV_HEAD_W, BF16, tm=512, tn=1024,
                                   name="mla_kv_expand_cache")
            cache_kpe = jnp.tile(cache_mla_kpe[:, j], (1, 1, LANES // QK_ROPE_DIM))
            o_ctx = _mla_ctx_call(q, kvx, kpe)
            o_lat = _mla_lat_call(q, kvx, kvx_cache, kpe, cache_kpe)
            w_o = mla_w_o
            ckv_out.append(ckv[:N_CTX].reshape(BATCH, SEQ, KV_LORA_RANK))
            kpe_out.append(kpe[:N_CTX, :QK_ROPE_DIM].reshape(BATCH, SEQ, QK_ROPE_DIM))
        o = jnp.concatenate([o_ctx, o_lat], axis=0)
        x = _resid_call(o, w_o, j, x, mod, layer, 2, tm=1024, tn=512, name="mixer_out_proj")
        h = _modulate_call(x, norm_ffn, layer, mod, 3, 4)
        act = _swiglu_call(h, ffn_w_gate, ffn_w_up, layer)
        x = _resid_call(act, ffn_w_down, layer, x, mod, layer, 5, tm=1024, tn=256, name="ffn_down_proj")

    y_prompt = _final_norm_call(x, norm_final, 0, N_CTX).reshape(BATCH, SEQ, D_MODEL)
    y_sample = _final_norm_call(x, norm_final, N_CTX, N_LAT).reshape(DEC_BATCH, DEC_SEQ, D_MODEL)
    return (y_prompt, y_sample,
            jnp.stack(win_k_out, axis=1), jnp.stack(win_v_out, axis=1),
            jnp.stack(ckv_out, axis=1), jnp.stack(kpe_out, axis=1))
```

```python
import functools
import math

import numpy as np
import jax
import jax.numpy as jnp
from jax import lax
from jax.experimental import pallas as pl
from jax.experimental.pallas import tpu as pltpu

F32 = jnp.float32
BF16 = jnp.bfloat16

D_MODEL = 2048
BATCH = 16
SEQ = 256
DEPTH = 4
DEC_BATCH = 2
DEC_SEQ = 2048
PAST_LEN = 256
GRID_W = 64
N_WIN_LAYERS = 2
N_MLA_LAYERS = 2
WIN_HEADS = 32
WIN_KV_HEADS = 8
WIN_GROUP = WIN_HEADS // WIN_KV_HEADS
WIN_HEAD_DIM = 64
WINDOW = 128
BLOCK = 128
WIN_SCALE = WIN_HEAD_DIM ** -0.5
MLA_HEADS = 16
Q_LORA_RANK = 512
KV_LORA_RANK = 512
QK_NOPE_DIM = 128
QK_ROPE_DIM = 64
V_HEAD_DIM = 128
MLA_SCALE = (QK_NOPE_DIM + QK_ROPE_DIM) ** -0.5
D_FF = 5632
ROPE_BASE = 10000.0
EPS = 1e-6
NEG = float(np.finfo(np.float32).min)
LOG2E = math.log2(math.e)

N_CTX = BATCH * SEQ
N_LAT = DEC_BATCH * DEC_SEQ
M_ALL = N_CTX + N_LAT
MOD_ROWS = 16
LANES = 128
HALF_LANES = LANES // 2
WIN_QD = WIN_HEADS * WIN_HEAD_DIM
WIN_KD = WIN_KV_HEADS * WIN_HEAD_DIM
MLA_IN_EXT = Q_LORA_RANK + KV_LORA_RANK + LANES
VMEM_LIMIT = 56 * 1024 * 1024


def _params(n_axes, vmem=VMEM_LIMIT):
    return pltpu.CompilerParams(dimension_semantics=("arbitrary",) * n_axes,
                                vmem_limit_bytes=vmem)


def _group(i, tm):
    start = i * tm
    return jnp.where(start < N_CTX, 0, 1 + (start - N_CTX) // DEC_SEQ)


def _mod_spec(layer, chunk, tm, tn, m_axis, n_axis):
    nb = D_MODEL // tn

    def imap(*ids):
        j = 0 if n_axis is None else ids[n_axis]
        return (layer * MOD_ROWS + _group(ids[m_axis], tm), 0, chunk * nb + j)

    return pl.BlockSpec((None, 1, tn), imap)


def _rms(x, g):
    y = x * lax.rsqrt(jnp.mean(x * x, axis=-1, keepdims=True) + EPS)
    return y * g


def _dot(a, b):
    return jnp.dot(a, b, preferred_element_type=F32)


def _dot_nt(a, b):
    return lax.dot_general(a, b, (((1,), (1,)), ((), ())), preferred_element_type=F32)


def _rope(x, c, s):
    lane = lax.broadcasted_iota(jnp.int32, (x.shape[0], LANES), 1)
    first = (lane & 31) < 16
    outs = []
    for k in range(x.shape[1] // LANES):
        a = x[:, k * LANES:(k + 1) * LANES]
        partner = jnp.where(first, pltpu.roll(a, LANES - 16, 1), pltpu.roll(a, 16, 1))
        outs.append(a * c + partner * s)
    return outs[0] if len(outs) == 1 else jnp.concatenate(outs, axis=1)


def _ada_body(cond_ref, w_ref, b_ref, o_ref):
    s = jax.nn.silu(cond_ref[...])
    o_ref[...] = _dot(s.astype(BF16), w_ref[...].astype(BF16)) + b_ref[...]


def _ada_call(cond, ada_w, ada_b):
    tn = 1024
    n = 6 * D_MODEL
    return pl.pallas_call(
        _ada_body,
        grid=(DEPTH, n // tn),
        in_specs=[
            pl.BlockSpec((MOD_ROWS, D_MODEL), lambda l, j: (0, 0)),
            pl.BlockSpec((None, D_MODEL, tn), lambda l, j: (l, 0, j)),
            pl.BlockSpec((None, 1, tn), lambda l, j: (l, 0, j)),
        ],
        out_specs=pl.BlockSpec((None, MOD_ROWS, tn), lambda l, j: (l, 0, j)),
        out_shape=jax.ShapeDtypeStruct((DEPTH, MOD_ROWS, n), F32),
        compiler_params=_params(2),
        name="ada_modulation",
    )(cond, ada_w, ada_b.reshape(DEPTH, 1, n))


def _modulate_body(x_ref, g_ref, sc_ref, sh_ref, o_ref):
    y = _rms(x_ref[...], g_ref[...])
    o_ref[...] = (y * (1.0 + sc_ref[...]) + sh_ref[...]).astype(o_ref.dtype)


def _modulate_call(x, gains, layer, mod, shift_chunk, scale_chunk):
    tm = 512
    return pl.pallas_call(
        _modulate_body,
        grid=(M_ALL // tm,),
        in_specs=[
            pl.BlockSpec((tm, D_MODEL), lambda i: (i, 0)),
            pl.BlockSpec((None, 1, D_MODEL), lambda i: (layer, 0, 0)),
            _mod_spec(layer, scale_chunk, tm, D_MODEL, 0, None),
            _mod_spec(layer, shift_chunk, tm, D_MODEL, 0, None),
        ],
        out_specs=pl.BlockSpec((tm, D_MODEL), lambda i: (i, 0)),
        out_shape=jax.ShapeDtypeStruct((M_ALL, D_MODEL), BF16),
        compiler_params=_params(1),
        name="norm_modulate",
    )(x, gains.reshape(DEPTH, 1, D_MODEL), mod, mod)


def _final_norm_body(x_ref, g_ref, o_ref):
    o_ref[...] = _rms(x_ref[...], g_ref[...])


def _final_norm_call(x, gain, row_off, n_rows):
    tm = 512
    off = row_off // tm
    return pl.pallas_call(
        _final_norm_body,
        grid=(n_rows // tm,),
        in_specs=[
            pl.BlockSpec((tm, D_MODEL), lambda i: (off + i, 0)),
            pl.BlockSpec((1, D_MODEL), lambda i: (0, 0)),
        ],
        out_specs=pl.BlockSpec((tm, D_MODEL), lambda i: (i, 0)),
        out_shape=jax.ShapeDtypeStruct((n_rows, D_MODEL), F32),
        compiler_params=_params(1),
        name="final_norm",
    )(x, gain.reshape(1, D_MODEL))


def _proj_body(*refs, rope_lo, rope_hi, n_ctx_tiles):
    has_rope = rope_hi > rope_lo
    if has_rope:
        lhs_ref, w_ref, cos_ref, sin_ref, o_ref, w_s = refs
    else:
        lhs_ref, w_ref, o_ref, w_s = refs
    j = pl.program_id(0)
    i = pl.program_id(1)

    @pl.when(i == 0)
    def _():
        w_s[...] = w_ref[...].astype(BF16)

    acc = _dot(lhs_ref[...].astype(BF16), w_s[...])
    if not has_rope:
        o_ref[...] = acc.astype(o_ref.dtype)
    else:
        roped = jnp.logical_and(jnp.logical_and(j >= rope_lo, j < rope_hi), i >= n_ctx_tiles)

        @pl.when(roped)
        def _():
            o_ref[...] = _rope(acc, cos_ref[...], sin_ref[...]).astype(o_ref.dtype)

        @pl.when(jnp.logical_not(roped))
        def _():
            o_ref[...] = acc.astype(o_ref.dtype)


def _proj_call(lhs, w, layer, col_off, n_cols, out_dtype, *, tm, tn, rope=None, tables=None, name):
    m, k = lhs.shape
    n_tiles = n_cols // tn
    off = col_off // tn
    rope_lo, rope_hi = rope if rope is not None else (0, 0)
    if w.ndim == 3:
        w_spec = pl.BlockSpec((None, k, tn), lambda j, i: (layer, 0, off + j))
    else:
        w_spec = pl.BlockSpec((k, tn), lambda j, i: (0, off + j))
    in_specs = [pl.BlockSpec((tm, k), lambda j, i: (i, 0)), w_spec]
    args = [lhs, w]
    if rope_hi > rope_lo:
        in_specs += [pl.BlockSpec((tm, LANES), lambda j, i: (i, 0))] * 2
        args += list(tables)
    return pl.pallas_call(
        functools.partial(_proj_body, rope_lo=rope_lo, rope_hi=rope_hi, n_ctx_tiles=N_CTX // tm),
        grid=(n_tiles, m // tm),
        in_specs=in_specs,
        out_specs=pl.BlockSpec((tm, tn), lambda j, i: (i, j)),
        out_shape=jax.ShapeDtypeStruct((m, n_cols), out_dtype),
        scratch_shapes=[pltpu.VMEM((k, tn), BF16)],
        compiler_params=_params(2),
        name=name,
    )(*args)


def _resid_body(*refs, n_parts, part_tiles):
    lhs_refs = refs[:n_parts]
    w_ref, x_ref, gate_ref, o_ref, w_s = refs[n_parts:]
    i = pl.program_id(1)

    @pl.when(i == 0)
    def _():
        w_s[...] = w_ref[...].astype(BF16)

    def run(lhs_ref):
        o_ref[...] = x_ref[...] + gate_ref[...] * _dot(lhs_ref[...], w_s[...])

    if n_parts == 1:
        run(lhs_refs[0])
    else:
        pl.when(i < part_tiles)(lambda: run(lhs_refs[0]))
        pl.when(i >= part_tiles)(lambda: run(lhs_refs[1]))


def _resid_call(lhs_parts, w, layer, x, mod, mod_layer, gate_chunk, *, tm, tn, name):
    k = lhs_parts[0].shape[1]
    m = sum(p.shape[0] for p in lhs_parts)
    n_parts = len(lhs_parts)
    part_tiles = lhs_parts[0].shape[0] // tm
    if n_parts == 1:
        lhs_specs = [pl.BlockSpec((tm, k), lambda j, i: (i, 0))]
    else:
        lhs_specs = [
            pl.BlockSpec((tm, k), lambda j, i: (jnp.minimum(i, part_tiles - 1), 0)),
            pl.BlockSpec((tm, k), lambda j, i: (jnp.maximum(i - part_tiles, 0), 0)),
        ]
    if w.ndim == 3:
        w_spec = pl.BlockSpec((None, k, tn), lambda j, i: (layer, 0, j))
    else:
        w_spec = pl.BlockSpec((k, tn), lambda j, i: (0, j))
    return pl.pallas_call(
        functools.partial(_resid_body, n_parts=n_parts, part_tiles=part_tiles),
        grid=(D_MODEL // tn, m // tm),
        in_specs=lhs_specs + [
            w_spec,
            pl.BlockSpec((tm, tn), lambda j, i: (i, j)),
            _mod_spec(mod_layer, gate_chunk, tm, tn, 1, 0),
        ],
        out_specs=pl.BlockSpec((tm, tn), lambda j, i: (i, j)),
        out_shape=jax.ShapeDtypeStruct((m, D_MODEL), F32),
        scratch_shapes=[pltpu.VMEM((k, tn), BF16)],
        compiler_params=_params(2),
        name=name,
    )(*lhs_parts, w, x, mod)


def _swiglu_body(h_ref, wg_ref, wu_ref, o_ref, wg_s, wu_s):
    @pl.when(pl.program_id(1) == 0)
    def _():
        wg_s[...] = wg_ref[...].astype(BF16)
        wu_s[...] = wu_ref[...].astype(BF16)

    h = h_ref[...]
    g = _dot(h, wg_s[...])
    u = _dot(h, wu_s[...])
    o_ref[...] = (jax.nn.silu(g) * u).astype(o_ref.dtype)


def _swiglu_call(h, w_gate, w_up, layer):
    tm, tn = 1024, 512
    return pl.pallas_call(
        _swiglu_body,
        grid=(D_FF // tn, M_ALL // tm),
        in_specs=[
            pl.BlockSpec((tm, D_MODEL), lambda j, i: (i, 0)),
            pl.BlockSpec((None, D_MODEL, tn), lambda j, i: (layer, 0, j)),
            pl.BlockSpec((None, D_MODEL, tn), lambda j, i: (layer, 0, j)),
        ],
        out_specs=pl.BlockSpec((tm, tn), lambda j, i: (i, j)),
        out_shape=jax.ShapeDtypeStruct((M_ALL, D_FF), BF16),
        scratch_shapes=[pltpu.VMEM((D_MODEL, tn), BF16), pltpu.VMEM((D_MODEL, tn), BF16)],
        compiler_params=_params(2),
        name="ffn_gate_up",
    )(h, w_gate, w_up)


def _mla_in_body(h_ref, w_ref, qn_ref, kvn_ref, cos_ref, sin_ref, cq_ref, ckv_ref, kpe_ref, w_s):
    @pl.when(pl.program_id(0) == 0)
    def _():
        w_s[...] = w_ref[...].astype(BF16)

    acc = _dot(h_ref[...], w_s[...])
    cq = acc[:, :Q_LORA_RANK]
    ckv = acc[:, Q_LORA_RANK:Q_LORA_RANK + KV_LORA_RANK]
    kpe = acc[:, Q_LORA_RANK + KV_LORA_RANK:]
    cq_ref[...] = _rms(cq, qn_ref[...]).astype(cq_ref.dtype)
    ckv_ref[...] = _rms(ckv, kvn_ref[...])
    kpe_ref[...] = _rope(kpe, cos_ref[...], sin_ref[...])


def _mla_in_call(h, w_ext, q_norm, kv_norm, tables):
    tm = 1024
    row = lambda i: (i, 0)
    fixed = lambda i: (0, 0)
    return pl.pallas_call(
        _mla_in_body,
        grid=(M_ALL // tm,),
        in_specs=[
            pl.BlockSpec((tm, D_MODEL), row),
            pl.BlockSpec((D_MODEL, MLA_IN_EXT), fixed),
            pl.BlockSpec((1, Q_LORA_RANK), fixed),
            pl.BlockSpec((1, KV_LORA_RANK), fixed),
            pl.BlockSpec((tm, LANES), row),
            pl.BlockSpec((tm, LANES), row),
        ],
        out_specs=[
            pl.BlockSpec((tm, Q_LORA_RANK), row),
            pl.BlockSpec((tm, KV_LORA_RANK), row),
            pl.BlockSpec((tm, LANES), row),
        ],
        out_shape=[
            jax.ShapeDtypeStruct((M_ALL, Q_LORA_RANK), BF16),
            jax.ShapeDtypeStruct((M_ALL, KV_LORA_RANK), F32),
            jax.ShapeDtypeStruct((M_ALL, LANES), F32),
        ],
        scratch_shapes=[pltpu.VMEM((D_MODEL, MLA_IN_EXT), BF16)],
        compiler_params=_params(1),
        name="mla_in_proj",
    )(h, w_ext, q_norm.reshape(1, -1), kv_norm.reshape(1, -1), *tables)


def _low_lanes(rows):
    return lax.broadcasted_iota(jnp.int32, (rows, LANES), 1) < HALF_LANES


def _win_heads(q_ref, o_ref, sink_ref, layer, k_chunks, v_chunks, blocks, nq):
    low_q = _low_lanes(nq)
    qlo = jnp.where(low_q, WIN_SCALE, 0.0).astype(BF16)
    qhi = jnp.where(low_q, 0.0, WIN_SCALE).astype(BF16)
    heads_per_pair = 2 * WIN_GROUP
    for p in range(WIN_KV_HEADS // 2):
        kc = k_chunks[p].astype(BF16)
        vc = v_chunks[p].astype(BF16)
        qcs = [q_ref[:, (WIN_GROUP * p + t) * LANES:(WIN_GROUP * p + t + 1) * LANES] for t in range(WIN_GROUP)]
        qs = jnp.concatenate([qc * qlo for qc in qcs] + [qc * qhi for qc in qcs], axis=0)
        lg = _dot_nt(qs, kc)
        probs, invs = [], []
        for u in range(heads_per_pair):
            sk = sink_ref[layer, heads_per_pair * p + u]
            l = lg[u * nq:(u + 1) * nq]
            cols = [l[:, a:a + LANES] if msk is None else jnp.where(msk, l[:, a:a + LANES], NEG)
                    for a, msk in blocks]
            mx = cols[0]
            for col in cols[1:]:
                mx = jnp.maximum(mx, col)
            m = jnp.maximum(jnp.max(mx, axis=-1, keepdims=True), sk)
            es = [jnp.exp(col - m) for col in cols]
            tot = es[0]
            for e in es[1:]:
                tot = tot + e
            invs.append(1.0 / (jnp.sum(tot, axis=-1, keepdims=True) + jnp.exp(sk - m)))
            probs.append(jnp.concatenate([e.astype(BF16) for e in es], axis=1))
        o = _dot(jnp.concatenate(probs, axis=0), vc)
        on = [o[u * nq:(u + 1) * nq] * invs[u] for u in range(heads_per_pair)]
        for t in range(WIN_GROUP):
            c = WIN_GROUP * p + t
            o_ref[:, c * LANES:(c + 1) * LANES] = jnp.where(low_q, on[t], on[WIN_GROUP + t]).astype(o_ref.dtype)


def _win_ctx_body(sink_ref, q_ref, kv_ref, o_ref, *, layer):
    n_pairs = WIN_KD // LANES
    k_chunks = [kv_ref[:, p * LANES:(p + 1) * LANES] for p in range(n_pairs)]
    v_chunks = [kv_ref[:, WIN_KD + p * LANES:WIN_KD + (p + 1) * LANES] for p in range(n_pairs)]
    _win_heads(q_ref, o_ref, sink_ref, layer, k_chunks, v_chunks,
               [(a, None) for a in range(0, SEQ, LANES)], SEQ)


def _win_ctx_call(q, kv, sink, layer):
    return pl.pallas_call(
        functools.partial(_win_ctx_body, layer=layer),
        grid=(BATCH,),
        in_specs=[
            pl.BlockSpec(memory_space=pltpu.SMEM),
            pl.BlockSpec((SEQ, WIN_QD), lambda b: (b, 0)),
            pl.BlockSpec((SEQ, 2 * WIN_KD), lambda b: (b, 0)),
        ],
        out_specs=pl.BlockSpec((SEQ, WIN_QD), lambda b: (b, 0)),
        out_shape=jax.ShapeDtypeStruct((N_CTX, WIN_QD), BF16),
        compiler_params=_params(1),
        name="win_attn_context",
    )(sink, q, kv)


def _win_lat_body(sink_ref, q_ref, kvp_ref, kvc_ref, kvn_ref, ck_ref, cv_ref, o_ref, *, layer):
    n = pl.program_id(1)
    n_pairs = WIN_KD // LANES
    k_chunks, v_chunks = [], []
    for p in range(n_pairs):
        ks = slice(p * LANES, (p + 1) * LANES)
        vs = slice(WIN_KD + p * LANES, WIN_KD + (p + 1) * LANES)
        k_chunks.append(jnp.concatenate([kvp_ref[:, ks], kvc_ref[:, ks], kvn_ref[:, ks], ck_ref[:, ks]], axis=0))
        v_chunks.append(jnp.concatenate([kvp_ref[:, vs], kvc_ref[:, vs], kvn_ref[:, vs], cv_ref[:, ks]], axis=0))
    r = lax.broadcasted_iota(jnp.int32, (BLOCK, BLOCK), 0)
    s = lax.broadcasted_iota(jnp.int32, (BLOCK, BLOCK), 1)
    prev_ok = (s - r) >= jnp.where(n >= 1, 0, BLOCK)
    next_ok = (r - s) >= jnp.where(n <= DEC_SEQ // BLOCK - 2, 0, BLOCK)
    blocks = [(0, prev_ok), (BLOCK, None), (2 * BLOCK, next_ok)]
    blocks += [(3 * BLOCK + a, None) for a in range(0, PAST_LEN, LANES)]
    _win_heads(q_ref, o_ref, sink_ref, layer, k_chunks, v_chunks, blocks, BLOCK)


def _win_lat_call(q, kv, cache_k, cache_v, sink, layer):
    nb = DEC_SEQ // BLOCK
    base = N_CTX // BLOCK

    def rows(shift):
        return lambda b, n: (base + b * nb + jnp.clip(n + shift, 0, nb - 1), 0)

    cache_spec = pl.BlockSpec((None, None, PAST_LEN, WIN_KD), lambda b, n: (b, layer, 0, 0))
    return pl.pallas_call(
        functools.partial(_win_lat_body, layer=layer),
        grid=(DEC_BATCH, nb),
        in_specs=[
            pl.BlockSpec(memory_space=pltpu.SMEM),
            pl.BlockSpec((BLOCK, WIN_QD), rows(0)),
            pl.BlockSpec((BLOCK, 2 * WIN_KD), rows(-1)),
            pl.BlockSpec((BLOCK, 2 * WIN_KD), rows(0)),
            pl.BlockSpec((BLOCK, 2 * WIN_KD), rows(1)),
            cache_spec,
            cache_spec,
        ],
        out_specs=pl.BlockSpec((BLOCK, WIN_QD), lambda b, n: (b * nb + n, 0)),
        out_shape=jax.ShapeDtypeStruct((N_LAT, WIN_QD), BF16),
        compiler_params=_params(2),
        name="win_attn_latent",
    )(sink, q, kv, kv, kv, cache_k, cache_v)


KV_HEAD_W = QK_NOPE_DIM + V_HEAD_DIM
MLA_NOPE_W = MLA_HEADS * QK_NOPE_DIM
MLA_ROPE_W = MLA_HEADS * QK_ROPE_DIM
MLA_EXP2_SCALE = MLA_SCALE * LOG2E


def _half_masks_bf16(rows):
    low = _low_lanes(rows)
    return jnp.where(low, 1.0, 0.0).astype(BF16), jnp.where(low, 0.0, 1.0).astype(BF16)


def _mla_ctx_body(qn_ref, qp_ref, kvx_ref, kpe_ref, o_ref):
    qlo, qhi = _half_masks_bf16(SEQ)
    kpe2 = kpe_ref[...].astype(BF16)
    for h in range(MLA_HEADS):
        qp = qp_ref[:, (h // 2) * LANES:(h // 2 + 1) * LANES] * (qlo if h % 2 == 0 else qhi)
        qcat = jnp.concatenate([qn_ref[:, h * LANES:(h + 1) * LANES], qp], axis=1)
        kcat = jnp.concatenate([kvx_ref[:, h * KV_HEAD_W:h * KV_HEAD_W + QK_NOPE_DIM], kpe2], axis=1)
        lg = _dot_nt(qcat, kcat)
        m = jnp.max(lg, axis=-1, keepdims=True)
        e = jnp.exp2((lg - m) * MLA_EXP2_SCALE)
        inv = 1.0 / jnp.sum(e, axis=-1, keepdims=True)
        o = _dot(e.astype(BF16), kvx_ref[:, h * KV_HEAD_W + QK_NOPE_DIM:(h + 1) * KV_HEAD_W])
        o_ref[:, h * LANES:(h + 1) * LANES] = (o * inv).astype(o_ref.dtype)


def _mla_ctx_call(q, kvx, kpe):
    return pl.pallas_call(
        _mla_ctx_body,
        grid=(BATCH,),
        in_specs=[
            pl.BlockSpec((SEQ, MLA_NOPE_W), lambda b: (b, 0)),
            pl.BlockSpec((SEQ, MLA_ROPE_W), lambda b: (b, MLA_NOPE_W // MLA_ROPE_W)),
            pl.BlockSpec((SEQ, MLA_HEADS * KV_HEAD_W), lambda b: (b, 0)),
            pl.BlockSpec((SEQ, LANES), lambda b: (b, 0)),
        ],
        out_specs=pl.BlockSpec((SEQ, MLA_HEADS * V_HEAD_DIM), lambda b: (b, 0)),
        out_shape=jax.ShapeDtypeStruct((N_CTX, MLA_HEADS * V_HEAD_DIM), BF16),
        compiler_params=_params(1),
        name="mla_attn_context",
    )(q, q, kvx, kpe)


MLA_QB = 512


def _mla_lat_body(qn_ref, qp_ref, kvl_ref, kvc_ref, kpl_ref, kpc_ref, o_ref):
    qlo, qhi = _half_masks_bf16(MLA_QB)
    kpl = kpl_ref[...].astype(BF16)
    kpc = kpc_ref[...].astype(BF16)
    qp_pair = qp_ref[...]
    for t in range(2):
        c0 = t * KV_HEAD_W
        qcat = jnp.concatenate([qn_ref[:, t * LANES:(t + 1) * LANES], qp_pair * (qlo if t == 0 else qhi)], axis=1)
        k_lat = jnp.concatenate([kvl_ref[:, c0:c0 + QK_NOPE_DIM], kpl], axis=1)
        k_cache = jnp.concatenate([kvc_ref[:, c0:c0 + QK_NOPE_DIM], kpc], axis=1)
        l1 = _dot_nt(qcat, k_lat)
        l2 = _dot_nt(qcat, k_cache)
        m = jnp.maximum(jnp.max(l1, axis=-1, keepdims=True), jnp.max(l2, axis=-1, keepdims=True))
        e1 = jnp.exp2((l1 - m) * MLA_EXP2_SCALE)
        e2 = jnp.exp2((l2 - m) * MLA_EXP2_SCALE)
        inv = 1.0 / (jnp.sum(e1, axis=-1, keepdims=True) + jnp.sum(e2, axis=-1, keepdims=True))
        o = (_dot(e1.astype(BF16), kvl_ref[:, c0 + QK_NOPE_DIM:c0 + KV_HEAD_W])
             + _dot(e2.astype(BF16), kvc_ref[:, c0 + QK_NOPE_DIM:c0 + KV_HEAD_W]))
        o_ref[:, t * LANES:(t + 1) * LANES] = (o * inv).astype(o_ref.dtype)


def _mla_lat_call(q, kvx, kvx_cache, kpe, kpe_cache):
    n_qb = DEC_SEQ // MLA_QB
    n_pairs = MLA_HEADS // 2
    lat_blk = N_CTX // DEC_SEQ
    return pl.pallas_call(
        _mla_lat_body,
        grid=(DEC_BATCH, n_pairs, n_qb),
        in_specs=[
            pl.BlockSpec((MLA_QB, 2 * QK_NOPE_DIM), lambda b, hp, qb: (N_CTX // MLA_QB + b * n_qb + qb, hp)),
            pl.BlockSpec((MLA_QB, LANES), lambda b, hp, qb: (N_CTX // MLA_QB + b * n_qb + qb, MLA_NOPE_W // LANES + hp)),
            pl.BlockSpec((DEC_SEQ, 2 * KV_HEAD_W), lambda b, hp, qb: (lat_blk + b, hp)),
            pl.BlockSpec((PAST_LEN, 2 * KV_HEAD_W), lambda b, hp, qb: (b, hp)),
            pl.BlockSpec((DEC_SEQ, LANES), lambda b, hp, qb: (lat_blk + b, 0)),
            pl.BlockSpec((None, PAST_LEN, LANES), lambda b, hp, qb: (b, 0, 0)),
        ],
        out_specs=pl.BlockSpec((MLA_QB, 2 * V_HEAD_DIM), lambda b, hp, qb: (b * n_qb + qb, hp)),
        out_shape=jax.ShapeDtypeStruct((N_LAT, MLA_HEADS * V_HEAD_DIM), BF16),
        compiler_params=_params(3),
        name="mla_attn_latent",
    )(q, q, kvx, kvx_cache, kpe, kpe_cache)


def _rope_tables():
    t = jnp.arange(DEC_SEQ, dtype=jnp.int32)
    rows, cols = t // GRID_W, t % GRID_W
    half = WIN_HEAD_DIM // 4
    freqs = ROPE_BASE ** (-jnp.arange(half, dtype=F32) / half)
    ang_r = rows.astype(F32)[:, None] * freqs[None, :]
    ang_c = cols.astype(F32)[:, None] * freqs[None, :]
    cr, sr, cc, sc = jnp.cos(ang_r), jnp.sin(ang_r), jnp.cos(ang_c), jnp.sin(ang_c)
    c64 = jnp.concatenate([cr, cr, cc, cc], axis=-1)
    s64 = jnp.concatenate([-sr, sr, -sc, sc], axis=-1)
    c_lat = jnp.tile(c64, (DEC_BATCH, LANES // WIN_HEAD_DIM))
    s_lat = jnp.tile(s64, (DEC_BATCH, LANES // WIN_HEAD_DIM))
    cos_t = jnp.concatenate([jnp.ones((N_CTX, LANES), F32), c_lat], axis=0)
    sin_t = jnp.concatenate([jnp.zeros((N_CTX, LANES), F32), s_lat], axis=0)
    return cos_t, sin_t


def kernel(x_prompt, x_sample, cache_win_k, cache_win_v, cache_mla_ckv, cache_mla_kpe, c, c_ctx, ada_w, ada_b, norm_mix, norm_ffn, win_w_qkv, win_w_o, win_sink, mla_w_in, mla_q_norm, mla_w_q_b, mla_kv_norm, mla_w_kv_b, mla_w_o, ffn_w_gate, ffn_w_up, ffn_w_down, norm_final):
    x = jnp.concatenate([x_prompt.reshape(N_CTX, D_MODEL), x_sample.reshape(N_LAT, D_MODEL)], axis=0)
    cond = jnp.concatenate([c_ctx[None, :], c, jnp.zeros((MOD_ROWS - 1 - DEC_BATCH, D_MODEL), F32)], axis=0)
    mod = _ada_call(cond, ada_w, ada_b).reshape(DEPTH * MOD_ROWS, 1, 6 * D_MODEL)
    tables = _rope_tables()
    cache_k = cache_win_k.reshape(DEC_BATCH, N_WIN_LAYERS, PAST_LEN, WIN_KD)
    cache_v = cache_win_v.reshape(DEC_BATCH, N_WIN_LAYERS, PAST_LEN, WIN_KD)

    win_k_out, win_v_out, ckv_out, kpe_out = [], [], [], []
    for layer in range(DEPTH):
        j = layer // 2
        h = _modulate_call(x, norm_mix, layer, mod, 0, 1)
        if layer % 2 == 0:
            pair_shape = (WIN_KV_HEADS // 2, 2, WIN_GROUP, WIN_HEAD_DIM)
            w_q = win_w_qkv[j][:, :WIN_QD].reshape((D_MODEL,) + pair_shape)
            w_q = w_q.transpose(0, 1, 3, 2, 4).reshape(D_MODEL, WIN_QD)
            w_o = win_w_o[j].reshape(pair_shape + (D_MODEL,)).transpose(0, 2, 1, 3, 4).reshape(WIN_QD, D_MODEL)
            q = _proj_call(h, w_q, 0, 0, WIN_QD, BF16, tm=1024, tn=1024,
                           rope=(0, WIN_QD // 1024), tables=tables, name="win_q_proj")
            kv = _proj_call(h, win_w_qkv, j, WIN_QD, 2 * WIN_KD, F32, tm=1024, tn=512,
                            rope=(0, 1), tables=tables, name="win_kv_proj")
            o_ctx = _win_ctx_call(q, kv, win_sink, j)
            o_lat = _win_lat_call(q, kv, cache_k, cache_v, win_sink, j)
            win_k_out.append(kv[:N_CTX, :WIN_KD].reshape(BATCH, SEQ, WIN_KV_HEADS, WIN_HEAD_DIM))
            win_v_out.append(kv[:N_CTX, WIN_KD:].reshape(BATCH, SEQ, WIN_KV_HEADS, WIN_HEAD_DIM))
        else:
            w_in = mla_w_in[j]
            w_ext = jnp.concatenate([w_in, w_in[:, Q_LORA_RANK + KV_LORA_RANK:]], axis=1)
            cq, ckv, kpe = _mla_in_call(h, w_ext, mla_q_norm[j], mla_kv_norm[j], tables)
            wq = mla_w_q_b[j].reshape(Q_LORA_RANK, MLA_HEADS, QK_NOPE_DIM + QK_ROPE_DIM)
            wq = jnp.concatenate([wq[:, :, :QK_NOPE_DIM].reshape(Q_LORA_RANK, MLA_NOPE_W),
                                  wq[:, :, QK_NOPE_DIM:].reshape(Q_LORA_RANK, MLA_ROPE_W)], axis=1)
            q = _proj_call(cq, wq, 0, 0, MLA_NOPE_W + MLA_ROPE_W, BF16, tm=1024, tn=1024,
                           rope=(MLA_NOPE_W // 1024, (MLA_NOPE_W + MLA_ROPE_W) // 1024), tables=tables,
                           name="mla_q_proj")
            kvx = _proj_call(ckv, mla_w_kv_b, j, 0, MLA_HEADS * KV_HEAD_W, BF16, tm=1024, tn=1024,
                             name="mla_kv_expand")
            cache_ckv = cache_mla_ckv[:, j].reshape(DEC_BATCH * PAST_LEN, KV_LORA_RANK)
            kvx_cache = _proj_call(cache_ckv, mla_w_kv_b, j, 0, MLA_HEADS * KV_HEAD_W, BF16, tm=512, tn=1024,
                                   name="mla_kv_expand_cache")
            cache_kpe = jnp.tile(cache_mla_kpe[:, j], (1, 1, LANES // QK_ROPE_DIM))
            o_ctx = _mla_ctx_call(q, kvx, kpe)
            o_lat = _mla_lat_call(q, kvx, kvx_cache, kpe, cache_kpe)
            w_o = mla_w_o
            ckv_out.append(ckv[:N_CTX].reshape(BATCH, SEQ, KV_LORA_RANK))
            kpe_out.append(kpe[:N_CTX, :QK_ROPE_DIM].reshape(BATCH, SEQ, QK_ROPE_DIM))
        x = _resid_call([o_ctx, o_lat], w_o, j, x, mod, layer, 2, tm=512, tn=1024, name="mixer_out_proj")
        h = _modulate_call(x, norm_ffn, layer, mod, 3, 4)
        act = _swiglu_call(h, ffn_w_gate, ffn_w_up, layer)
        x = _resid_call([act], ffn_w_down, layer, x, mod, layer, 5, tm=512, tn=512, name="ffn_down_proj")

    y_prompt = _final_norm_call(x, norm_final, 0, N_CTX).reshape(BATCH, SEQ, D_MODEL)
    y_sample = _final_norm_call(x, norm_final, N_CTX, N_LAT).reshape(DEC_BATCH, DEC_SEQ, D_MODEL)
    return (y_prompt, y_sample,
            jnp.stack(win_k_out, axis=1), jnp.stack(win_v_out, axis=1),
            jnp.stack(ckv_out, axis=1), jnp.stack(kpe_out, axis=1))
```

```python
import functools
import math

import numpy as np
import jax
import jax.numpy as jnp
from jax import lax
from jax.experimental import pallas as pl
from jax.experimental.pallas import tpu as pltpu

F32 = jnp.float32
BF16 = jnp.bfloat16

D_MODEL = 2048
BATCH = 16
SEQ = 256
DEPTH = 4
DEC_BATCH = 2
DEC_SEQ = 2048
PAST_LEN = 256
GRID_W = 64
N_WIN_LAYERS = 2
N_MLA_LAYERS = 2
WIN_HEADS = 32
WIN_KV_HEADS = 8
WIN_GROUP = WIN_HEADS // WIN_KV_HEADS
WIN_HEAD_DIM = 64
WINDOW = 128
BLOCK = 128
WIN_SCALE = WIN_HEAD_DIM ** -0.5
MLA_HEADS = 16
Q_LORA_RANK = 512
KV_LORA_RANK = 512
QK_NOPE_DIM = 128
QK_ROPE_DIM = 64
V_HEAD_DIM = 128
MLA_SCALE = (QK_NOPE_DIM + QK_ROPE_DIM) ** -0.5
D_FF = 5632
ROPE_BASE = 10000.0
EPS = 1e-6
NEG = float(np.finfo(np.float32).min)
LOG2E = math.log2(math.e)

N_CTX = BATCH * SEQ
N_LAT = DEC_BATCH * DEC_SEQ
M_ALL = N_CTX + N_LAT
MOD_ROWS = 16
LANES = 128
HALF_LANES = LANES // 2
WIN_QD = WIN_HEADS * WIN_HEAD_DIM
WIN_KD = WIN_KV_HEADS * WIN_HEAD_DIM
VMEM_LIMIT = 56 * 1024 * 1024


def _params(n_axes, vmem=VMEM_LIMIT):
    return pltpu.CompilerParams(dimension_semantics=("arbitrary",) * n_axes,
                                vmem_limit_bytes=vmem)


def _group(i, tm):
    start = i * tm
    return jnp.where(start < N_CTX, 0, 1 + (start - N_CTX) // DEC_SEQ)


def _mod_spec(layer, chunk, tm, tn, m_axis, n_axis):
    nb = D_MODEL // tn

    def imap(*ids):
        j = 0 if n_axis is None else ids[n_axis]
        return (layer * MOD_ROWS + _group(ids[m_axis], tm), 0, chunk * nb + j)

    return pl.BlockSpec((None, 1, tn), imap)


def _rms(x, g):
    y = x * lax.rsqrt(jnp.mean(x * x, axis=-1, keepdims=True) + EPS)
    return y * g


def _dot(a, b):
    return jnp.dot(a, b, preferred_element_type=F32)


def _dot_nt(a, b):
    return lax.dot_general(a, b, (((1,), (1,)), ((), ())), preferred_element_type=F32)


def _rope(x, c, s):
    lane = lax.broadcasted_iota(jnp.int32, (x.shape[0], LANES), 1)
    first = (lane & 31) < 16
    outs = []
    for k in range(x.shape[1] // LANES):
        a = x[:, k * LANES:(k + 1) * LANES]
        partner = jnp.where(first, pltpu.roll(a, LANES - 16, 1), pltpu.roll(a, 16, 1))
        outs.append(a * c + partner * s)
    return outs[0] if len(outs) == 1 else jnp.concatenate(outs, axis=1)


def _ada_body(cond_ref, w_ref, b_ref, o_ref):
    s = jax.nn.silu(cond_ref[...])
    o_ref[...] = _dot(s.astype(BF16), w_ref[...].astype(BF16)) + b_ref[...]


def _ada_call(cond, ada_w, ada_b):
    tn = 1024
    n = 6 * D_MODEL
    return pl.pallas_call(
        _ada_body,
        grid=(DEPTH, n // tn),
        in_specs=[
            pl.BlockSpec((MOD_ROWS, D_MODEL), lambda l, j: (0, 0)),
            pl.BlockSpec((None, D_MODEL, tn), lambda l, j: (l, 0, j)),
            pl.BlockSpec((None, 1, tn), lambda l, j: (l, 0, j)),
        ],
        out_specs=pl.BlockSpec((None, MOD_ROWS, tn), lambda l, j: (l, 0, j)),
        out_shape=jax.ShapeDtypeStruct((DEPTH, MOD_ROWS, n), F32),
        compiler_params=_params(2),
        name="ada_modulation",
    )(cond, ada_w, ada_b.reshape(DEPTH, 1, n))


def _modulate_body(x_ref, g_ref, sc_ref, sh_ref, o_ref):
    y = _rms(x_ref[...], g_ref[...])
    o_ref[...] = (y * (1.0 + sc_ref[...]) + sh_ref[...]).astype(o_ref.dtype)


def _modulate_call(x, gains, layer, mod, shift_chunk, scale_chunk):
    tm = 512
    return pl.pallas_call(
        _modulate_body,
        grid=(M_ALL // tm,),
        in_specs=[
            pl.BlockSpec((tm, D_MODEL), lambda i: (i, 0)),
            pl.BlockSpec((None, 1, D_MODEL), lambda i: (layer, 0, 0)),
            _mod_spec(layer, scale_chunk, tm, D_MODEL, 0, None),
            _mod_spec(layer, shift_chunk, tm, D_MODEL, 0, None),
        ],
        out_specs=pl.BlockSpec((tm, D_MODEL), lambda i: (i, 0)),
        out_shape=jax.ShapeDtypeStruct((M_ALL, D_MODEL), BF16),
        compiler_params=_params(1),
        name="norm_modulate",
    )(x, gains.reshape(DEPTH, 1, D_MODEL), mod, mod)


def _low_lanes(rows):
    return lax.broadcasted_iota(jnp.int32, (rows, LANES), 1) < HALF_LANES


def _cast_pair_order(w_ref, w_s):
    low = _low_lanes(w_ref.shape[0])
    per_pair = 2 * WIN_GROUP * WIN_HEAD_DIM // LANES
    for p in range(w_ref.shape[1] // (per_pair * LANES)):
        c = [w_ref[:, (per_pair * p + t) * LANES:(per_pair * p + t + 1) * LANES] for t in range(per_pair)]
        r = [pltpu.roll(x, HALF_LANES, 1) for x in c]
        moved = [jnp.where(low, c[0], r[2]), jnp.where(low, r[0], c[2]),
                 jnp.where(low, c[1], r[3]), jnp.where(low, r[1], c[3])]
        for t, x in enumerate(moved):
            w_s[:, (per_pair * p + t) * LANES:(per_pair * p + t + 1) * LANES] = x.astype(BF16)


def _proj_body(*refs, rope_lo, rope_hi, n_ctx_tiles, pair_order):
    has_rope = rope_hi > rope_lo
    if has_rope:
        lhs_ref, w_ref, cos_ref, sin_ref, o_ref, w_s = refs
    else:
        lhs_ref, w_ref, o_ref, w_s = refs
    j = pl.program_id(0)
    i = pl.program_id(1)

    @pl.when(i == 0)
    def _():
        if pair_order:
            _cast_pair_order(w_ref, w_s)
        else:
            w_s[...] = w_ref[...].astype(BF16)

    acc = _dot(lhs_ref[...].astype(BF16), w_s[...])
    if not has_rope:
        o_ref[...] = acc.astype(o_ref.dtype)
    else:
        roped = jnp.logical_and(jnp.logical_and(j >= rope_lo, j < rope_hi), i >= n_ctx_tiles)

        @pl.when(roped)
        def _():
            o_ref[...] = _rope(acc, cos_ref[...], sin_ref[...]).astype(o_ref.dtype)

        @pl.when(jnp.logical_not(roped))
        def _():
            o_ref[...] = acc.astype(o_ref.dtype)


def _proj_call(lhs, w, layer, col_off, n_cols, out_dtype, *, tm, tn, rope=None, tables=None,
               pair_order=False, name):
    m, k = lhs.shape
    n_tiles = n_cols // tn
    off = col_off // tn
    rope_lo, rope_hi = rope if rope is not None else (0, 0)
    if w.ndim == 3:
        w_spec = pl.BlockSpec((None, k, tn), lambda j, i: (layer, 0, off + j))
    else:
        w_spec = pl.BlockSpec((k, tn), lambda j, i: (0, off + j))
    in_specs = [pl.BlockSpec((tm, k), lambda j, i: (i, 0)), w_spec]
    args = [lhs, w]
    if rope_hi > rope_lo:
        in_specs += [pl.BlockSpec((tm, LANES), lambda j, i: (i, 0))] * 2
        args += list(tables)
    return pl.pallas_call(
        functools.partial(_proj_body, rope_lo=rope_lo, rope_hi=rope_hi, n_ctx_tiles=N_CTX // tm,
                          pair_order=pair_order),
        grid=(n_tiles, m // tm),
        in_specs=in_specs,
        out_specs=pl.BlockSpec((tm, tn), lambda j, i: (i, j)),
        out_shape=jax.ShapeDtypeStruct((m, n_cols), out_dtype),
        scratch_shapes=[pltpu.VMEM((k, tn), BF16)],
        compiler_params=_params(2),
        name=name,
    )(*args)


ROW_BLOCK = WIN_HEAD_DIM


def _resid_norm_body(*refs, n_parts, ctx_tiles, cast_w, row_perm, final):
    lhs_refs = refs[:n_parts]
    n_in = 4 if final else 6
    ins, outs, scratch = refs[n_parts:n_parts + n_in], refs[n_parts + n_in:n_parts + n_in + 2], refs[n_parts + n_in + 2:]
    w_ref, x_ref, gate_ref, g_ref = ins[:4]
    i = pl.program_id(0)
    if cast_w:
        w = scratch[0]

        @pl.when(i == 0)
        def _():
            if row_perm is None:
                w[...] = w_ref[...].astype(BF16)
            else:
                for new, old in enumerate(row_perm):
                    w[new * ROW_BLOCK:(new + 1) * ROW_BLOCK, :] = (
                        w_ref[old * ROW_BLOCK:(old + 1) * ROW_BLOCK, :].astype(BF16))
    else:
        w = w_ref

    def step(lhs_ref, y_ref):
        xn = x_ref[...] + gate_ref[...] * _dot(lhs_ref[...], w[...])
        y = _rms(xn, g_ref[...])
        if final:
            y_ref[...] = y
        else:
            outs[0][...] = xn
            outs[1][...] = (y * (1.0 + ins[4][...]) + ins[5][...]).astype(outs[1].dtype)

    if n_parts == 1 and not final:
        step(lhs_refs[0], None)
    else:
        pl.when(i < ctx_tiles)(lambda: step(lhs_refs[0], outs[0]))
        pl.when(i >= ctx_tiles)(lambda: step(lhs_refs[-1], outs[1]))


def _resid_norm_call(lhs_parts, w, w_layer, x, mod, layer, gate_chunk, gains, gain_layer, next_mod, *,
                     tm, row_perm=None, name):
    k = lhs_parts[0].shape[1]
    n_parts = len(lhs_parts)
    ctx_tiles = N_CTX // tm
    final = next_mod is None
    cast_w = w.dtype != BF16
    row = lambda i: (i, 0)
    ctx_row = lambda i: (jnp.minimum(i, ctx_tiles - 1), 0)
    lat_row = lambda i: (jnp.maximum(i - ctx_tiles, 0), 0)
    if n_parts == 1:
        lhs_specs = [pl.BlockSpec((tm, k), row)]
    else:
        lhs_specs = [pl.BlockSpec((tm, k), ctx_row), pl.BlockSpec((tm, k), lat_row)]
    if w.ndim == 3:
        w_spec = pl.BlockSpec((None, k, D_MODEL), lambda i: (w_layer, 0, 0), pipeline_mode=pl.Buffered(1))
    else:
        w_spec = pl.BlockSpec((k, D_MODEL), lambda i: (0, 0))
    in_specs = lhs_specs + [
        w_spec,
        pl.BlockSpec((tm, D_MODEL), row),
        _mod_spec(layer, gate_chunk, tm, D_MODEL, 0, None),
        pl.BlockSpec((None, 1, D_MODEL), lambda i: (gain_layer, 0, 0)),
    ]
    args = list(lhs_parts) + [w, x, mod, gains]
    if final:
        out_specs = [pl.BlockSpec((tm, D_MODEL), ctx_row), pl.BlockSpec((tm, D_MODEL), lat_row)]
        out_shape = [jax.ShapeDtypeStruct((N_CTX, D_MODEL), F32), jax.ShapeDtypeStruct((N_LAT, D_MODEL), F32)]
    else:
        nl, shift_chunk, scale_chunk = next_mod
        in_specs += [_mod_spec(nl, scale_chunk, tm, D_MODEL, 0, None),
                     _mod_spec(nl, shift_chunk, tm, D_MODEL, 0, None)]
        args += [mod, mod]
        out_specs = [pl.BlockSpec((tm, D_MODEL), row), pl.BlockSpec((tm, D_MODEL), row)]
        out_shape = [jax.ShapeDtypeStruct((M_ALL, D_MODEL), F32), jax.ShapeDtypeStruct((M_ALL, D_MODEL), BF16)]
    return pl.pallas_call(
        functools.partial(_resid_norm_body, n_parts=n_parts, ctx_tiles=ctx_tiles, cast_w=cast_w,
                          row_perm=row_perm, final=final),
        grid=(M_ALL // tm,),
        in_specs=in_specs,
        out_specs=out_specs,
        out_shape=out_shape,
        scratch_shapes=[pltpu.VMEM((k, D_MODEL), BF16)] if cast_w else [],
        compiler_params=_params(1),
        name=name,
    )(*args)


def _swiglu_body(h_ref, wg_ref, wu_ref, wd_ref, o_ref, wdo_ref, wg_s, wu_s):
    @pl.when(pl.program_id(1) == 0)
    def _():
        wg_s[...] = wg_ref[...].astype(BF16)
        wu_s[...] = wu_ref[...].astype(BF16)

    h = h_ref[...]
    g = _dot(h, wg_s[...])
    u = _dot(h, wu_s[...])
    o_ref[...] = (jax.nn.silu(g) * u).astype(o_ref.dtype)
    wdo_ref[...] = wd_ref[...].astype(BF16)


def _swiglu_call(h, w_gate, w_up, w_down, layer):
    tm, tn = 1024, 512
    n_i = M_ALL // tm
    slab = D_FF // ((D_FF // tn) * n_i)
    return pl.pallas_call(
        _swiglu_body,
        grid=(D_FF // tn, n_i),
        in_specs=[
            pl.BlockSpec((tm, D_MODEL), lambda j, i: (i, 0)),
            pl.BlockSpec((None, D_MODEL, tn), lambda j, i: (layer, 0, j)),
            pl.BlockSpec((None, D_MODEL, tn), lambda j, i: (layer, 0, j)),
            pl.BlockSpec((None, slab, D_MODEL), lambda j, i: (layer, j * n_i + i, 0)),
        ],
        out_specs=[
            pl.BlockSpec((tm, tn), lambda j, i: (i, j)),
            pl.BlockSpec((slab, D_MODEL), lambda j, i: (j * n_i + i, 0)),
        ],
        out_shape=[jax.ShapeDtypeStruct((M_ALL, D_FF), BF16), jax.ShapeDtypeStruct((D_FF, D_MODEL), BF16)],
        scratch_shapes=[pltpu.VMEM((D_MODEL, tn), BF16), pltpu.VMEM((D_MODEL, tn), BF16)],
        compiler_params=_params(2),
        name="ffn_gate_up",
    )(h, w_gate, w_up, w_down)


def _mla_in_body(h_ref, w_ref, qn_ref, kvn_ref, cos_ref, sin_ref, cq_ref, ckv_ref, kpe_ref, w_s):
    @pl.when(pl.program_id(0) == 0)
    def _():
        w_s[...] = w_ref[...].astype(BF16)

    acc = _dot(h_ref[...], w_s[...])
    cq = acc[:, :Q_LORA_RANK]
    ckv = acc[:, Q_LORA_RANK:Q_LORA_RANK + KV_LORA_RANK]
    kpe = acc[:, Q_LORA_RANK + KV_LORA_RANK:]
    kpe = jnp.concatenate([kpe, kpe], axis=1)
    cq_ref[...] = _rms(cq, qn_ref[...]).astype(cq_ref.dtype)
    ckv_ref[...] = _rms(ckv, kvn_ref[...])
    kpe_ref[...] = _rope(kpe, cos_ref[...], sin_ref[...])


def _mla_in_call(h, w_in, layer, q_norm, kv_norm, tables):
    tm = 1024
    row = lambda i: (i, 0)
    fixed = lambda i: (0, 0)
    n_in = Q_LORA_RANK + KV_LORA_RANK + QK_ROPE_DIM
    return pl.pallas_call(
        _mla_in_body,
        grid=(M_ALL // tm,),
        in_specs=[
            pl.BlockSpec((tm, D_MODEL), row),
            pl.BlockSpec((None, D_MODEL, n_in), lambda i: (layer, 0, 0), pipeline_mode=pl.Buffered(1)),
            pl.BlockSpec((1, Q_LORA_RANK), fixed),
            pl.BlockSpec((1, KV_LORA_RANK), fixed),
            pl.BlockSpec((tm, LANES), row),
            pl.BlockSpec((tm, LANES), row),
        ],
        out_specs=[
            pl.BlockSpec((tm, Q_LORA_RANK), row),
            pl.BlockSpec((tm, KV_LORA_RANK), row),
            pl.BlockSpec((tm, LANES), row),
        ],
        out_shape=[
            jax.ShapeDtypeStruct((M_ALL, Q_LORA_RANK), BF16),
            jax.ShapeDtypeStruct((M_ALL, KV_LORA_RANK), F32),
            jax.ShapeDtypeStruct((M_ALL, LANES), F32),
        ],
        scratch_shapes=[pltpu.VMEM((D_MODEL, n_in), BF16)],
        compiler_params=_params(1),
        name="mla_in_proj",
    )(h, w_in, q_norm.reshape(1, -1), kv_norm.reshape(1, -1), *tables)


def _win_heads(q_ref, o_ref, sink_ref, layer, k_chunks, v_chunks, blocks, nq, pipelined):
    low_q = _low_lanes(nq)
    qlo = jnp.where(low_q, WIN_SCALE, 0.0).astype(BF16)
    qhi = jnp.where(low_q, 0.0, WIN_SCALE).astype(BF16)
    heads_per_pair = 2 * WIN_GROUP

    def scores(p):
        kc = k_chunks[p].astype(BF16)
        qcs = [q_ref[:, (WIN_GROUP * p + t) * LANES:(WIN_GROUP * p + t + 1) * LANES] for t in range(WIN_GROUP)]
        qs = jnp.concatenate([qc * qlo for qc in qcs] + [qc * qhi for qc in qcs], axis=0)
        return _dot_nt(qs, kc)

    def softmax(p, lg):
        probs, invs = [], []
        for u in range(heads_per_pair):
            sk = sink_ref[layer, heads_per_pair * p + u]
            l = lg[u * nq:(u + 1) * nq]
            cols = [l[:, a:a + LANES] if msk is None else jnp.where(msk, l[:, a:a + LANES], NEG)
                    for a, msk in blocks]
            mx = cols[0]
            for col in cols[1:]:
                mx = jnp.maximum(mx, col)
            m = jnp.maximum(jnp.max(mx, axis=-1, keepdims=True), sk)
            es = [jnp.exp(col - m) for col in cols]
            tot = es[0]
            for e in es[1:]:
                tot = tot + e
            invs.append(1.0 / (jnp.sum(tot, axis=-1, keepdims=True) + jnp.exp(sk - m)))
            probs.append(jnp.concatenate([e.astype(BF16) for e in es], axis=1))
        return jnp.concatenate(probs, axis=0), invs

    def values(p, probs, invs):
        o = _dot(probs, v_chunks[p].astype(BF16))
        on = [o[u * nq:(u + 1) * nq] * invs[u] for u in range(heads_per_pair)]
        for t in range(WIN_GROUP):
            c = WIN_GROUP * p + t
            o_ref[:, c * LANES:(c + 1) * LANES] = jnp.where(low_q, on[t], on[WIN_GROUP + t]).astype(o_ref.dtype)

    n = WIN_KV_HEADS // 2
    if not pipelined:
        for p in range(n):
            values(p, *softmax(p, scores(p)))
        return
    sc, pr = {}, {}
    for p in range(n + 2):
        if p < n:
            sc[p] = scores(p)
        if 0 <= p - 1 < n:
            pr[p - 1] = softmax(p - 1, sc.pop(p - 1))
        if 0 <= p - 2 < n:
            values(p - 2, *pr.pop(p - 2))


def _win_ctx_body(sink_ref, q_ref, kv_ref, o_ref, *, layer):
    n_pairs = WIN_KD // LANES
    k_chunks = [kv_ref[:, p * LANES:(p + 1) * LANES] for p in range(n_pairs)]
    v_chunks = [kv_ref[:, WIN_KD + p * LANES:WIN_KD + (p + 1) * LANES] for p in range(n_pairs)]
    _win_heads(q_ref, o_ref, sink_ref, layer, k_chunks, v_chunks,
               [(a, None) for a in range(0, SEQ, LANES)], SEQ, False)


def _win_ctx_call(q, kv, sink, layer):
    return pl.pallas_call(
        functools.partial(_win_ctx_body, layer=layer),
        grid=(BATCH,),
        in_specs=[
            pl.BlockSpec(memory_space=pltpu.SMEM),
            pl.BlockSpec((SEQ, WIN_QD), lambda b: (b, 0)),
            pl.BlockSpec((SEQ, 2 * WIN_KD), lambda b: (b, 0)),
        ],
        out_specs=pl.BlockSpec((SEQ, WIN_QD), lambda b: (b, 0)),
        out_shape=jax.ShapeDtypeStruct((N_CTX, WIN_QD), BF16),
        compiler_params=_params(1),
        name="win_attn_context",
    )(sink, q, kv)


def _win_lat_body(sink_ref, q_ref, kvp_ref, kvc_ref, kvn_ref, ck_ref, cv_ref, o_ref, *, layer):
    n = pl.program_id(1)
    n_pairs = WIN_KD // LANES
    k_chunks, v_chunks = [], []
    for p in range(n_pairs):
        ks = slice(p * LANES, (p + 1) * LANES)
        vs = slice(WIN_KD + p * LANES, WIN_KD + (p + 1) * LANES)
        k_chunks.append(jnp.concatenate([kvp_ref[:, ks], kvc_ref[:, ks], kvn_ref[:, ks], ck_ref[:, ks]], axis=0))
        v_chunks.append(jnp.concatenate([kvp_ref[:, vs], kvc_ref[:, vs], kvn_ref[:, vs], cv_ref[:, ks]], axis=0))
    r = lax.broadcasted_iota(jnp.int32, (BLOCK, BLOCK), 0)
    s = lax.broadcasted_iota(jnp.int32, (BLOCK, BLOCK), 1)
    prev_ok = (s - r) >= jnp.where(n >= 1, 0, BLOCK)
    next_ok = (r - s) >= jnp.where(n <= DEC_SEQ // BLOCK - 2, 0, BLOCK)
    blocks = [(0, prev_ok), (BLOCK, None), (2 * BLOCK, next_ok)]
    blocks += [(3 * BLOCK + a, None) for a in range(0, PAST_LEN, LANES)]
    _win_heads(q_ref, o_ref, sink_ref, layer, k_chunks, v_chunks, blocks, BLOCK, True)


def _win_lat_call(q, kv, cache_k, cache_v, sink, layer):
    nb = DEC_SEQ // BLOCK
    base = N_CTX // BLOCK

    def rows(shift):
        return lambda b, n: (base + b * nb + jnp.clip(n + shift, 0, nb - 1), 0)

    cache_spec = pl.BlockSpec((None, None, PAST_LEN, WIN_KD), lambda b, n: (b, layer, 0, 0))
    return pl.pallas_call(
        functools.partial(_win_lat_body, layer=layer),
        grid=(DEC_BATCH, nb),
        in_specs=[
            pl.BlockSpec(memory_space=pltpu.SMEM),
            pl.BlockSpec((BLOCK, WIN_QD), rows(0)),
            pl.BlockSpec((BLOCK, 2 * WIN_KD), rows(-1)),
            pl.BlockSpec((BLOCK, 2 * WIN_KD), rows(0)),
            pl.BlockSpec((BLOCK, 2 * WIN_KD), rows(1)),
            cache_spec,
            cache_spec,
        ],
        out_specs=pl.BlockSpec((BLOCK, WIN_QD), lambda b, n: (b * nb + n, 0)),
        out_shape=jax.ShapeDtypeStruct((N_LAT, WIN_QD), BF16),
        compiler_params=_params(2),
        name="win_attn_latent",
    )(sink, q, kv, kv, kv, cache_k, cache_v)


KV_HEAD_W = QK_NOPE_DIM + V_HEAD_DIM
MLA_NOPE_W = MLA_HEADS * QK_NOPE_DIM
MLA_ROPE_W = MLA_HEADS * QK_ROPE_DIM
MLA_EXP2_SCALE = MLA_SCALE * LOG2E


def _half_masks_bf16(rows):
    low = _low_lanes(rows)
    return jnp.where(low, 1.0, 0.0).astype(BF16), jnp.where(low, 0.0, 1.0).astype(BF16)


def _mla_ctx_body(qn_ref, qp_ref, kvx_ref, kpe_ref, o_ref):
    qlo, qhi = _half_masks_bf16(SEQ)
    kpe2 = kpe_ref[...].astype(BF16)
    for h in range(MLA_HEADS):
        qp = qp_ref[:, (h // 2) * LANES:(h // 2 + 1) * LANES] * (qlo if h % 2 == 0 else qhi)
        qcat = jnp.concatenate([qn_ref[:, h * LANES:(h + 1) * LANES], qp], axis=1)
        kcat = jnp.concatenate([kvx_ref[:, h * KV_HEAD_W:h * KV_HEAD_W + QK_NOPE_DIM], kpe2], axis=1)
        lg = _dot_nt(qcat, kcat)
        m = jnp.max(lg, axis=-1, keepdims=True)
        e = jnp.exp2((lg - m) * MLA_EXP2_SCALE)
        inv = 1.0 / jnp.sum(e, axis=-1, keepdims=True)
        o = _dot(e.astype(BF16), kvx_ref[:, h * KV_HEAD_W + QK_NOPE_DIM:(h + 1) * KV_HEAD_W])
        o_ref[:, h * LANES:(h + 1) * LANES] = (o * inv).astype(o_ref.dtype)


def _mla_ctx_call(q, kvx, kpe):
    return pl.pallas_call(
        _mla_ctx_body,
        grid=(BATCH,),
        in_specs=[
            pl.BlockSpec((SEQ, MLA_NOPE_W), lambda b: (b, 0)),
            pl.BlockSpec((SEQ, MLA_ROPE_W), lambda b: (b, MLA_NOPE_W // MLA_ROPE_W)),
            pl.BlockSpec((SEQ, MLA_HEADS * KV_HEAD_W), lambda b: (b, 0)),
            pl.BlockSpec((SEQ, LANES), lambda b: (b, 0)),
        ],
        out_specs=pl.BlockSpec((SEQ, MLA_HEADS * V_HEAD_DIM), lambda b: (b, 0)),
        out_shape=jax.ShapeDtypeStruct((N_CTX, MLA_HEADS * V_HEAD_DIM), BF16),
        compiler_params=_params(1),
        name="mla_attn_context",
    )(q, q, kvx, kpe)


MLA_QB = 2048
MLA_SUB = 512


def _mla_lat_body(qn_ref, qp_ref, kvl_ref, kvc_ref, kpl_ref, kpc_ref, o_ref):
    qlo, qhi = _half_masks_bf16(MLA_SUB)
    kpl = kpl_ref[...].astype(BF16)
    kpc = kpc_ref[...].astype(BF16)
    qp_pair = qp_ref[...]
    chains = [(t, r) for t in range(2) for r in range(MLA_QB // MLA_SUB)]

    def scores(t, r):
        c0 = t * KV_HEAD_W
        rows = slice(r * MLA_SUB, (r + 1) * MLA_SUB)
        qcat = jnp.concatenate([qn_ref[rows, t * LANES:(t + 1) * LANES],
                                qp_pair[rows] * (qlo if t == 0 else qhi)], axis=1)
        k_lat = jnp.concatenate([kvl_ref[:, c0:c0 + QK_NOPE_DIM], kpl], axis=1)
        k_cache = jnp.concatenate([kvc_ref[:, c0:c0 + QK_NOPE_DIM], kpc], axis=1)
        return _dot_nt(qcat, k_lat), _dot_nt(qcat, k_cache)

    def softmax(l1, l2):
        m = jnp.maximum(jnp.max(l1, axis=-1, keepdims=True), jnp.max(l2, axis=-1, keepdims=True))
        e1 = jnp.exp2((l1 - m) * MLA_EXP2_SCALE)
        e2 = jnp.exp2((l2 - m) * MLA_EXP2_SCALE)
        inv = 1.0 / (jnp.sum(e1, axis=-1, keepdims=True) + jnp.sum(e2, axis=-1, keepdims=True))
        return e1.astype(BF16), e2.astype(BF16), inv

    def values(t, r, p1, p2, inv):
        c0 = t * KV_HEAD_W
        o = (_dot(p1, kvl_ref[:, c0 + QK_NOPE_DIM:c0 + KV_HEAD_W])
             + _dot(p2, kvc_ref[:, c0 + QK_NOPE_DIM:c0 + KV_HEAD_W]))
        o_ref[r * MLA_SUB:(r + 1) * MLA_SUB, t * LANES:(t + 1) * LANES] = (o * inv).astype(o_ref.dtype)

    n = len(chains)
    sc, pr = {}, {}
    for c in range(n + 2):
        if c < n:
            sc[c] = scores(*chains[c])
        if 0 <= c - 1 < n:
            pr[c - 1] = softmax(*sc.pop(c - 1))
        if 0 <= c - 2 < n:
            values(*chains[c - 2], *pr.pop(c - 2))


def _mla_lat_call(q, kvx, kvx_cache, kpe, kpe_cache):
    n_qb = DEC_SEQ // MLA_QB
    n_pairs = MLA_HEADS // 2
    lat_blk = N_CTX // DEC_SEQ
    return pl.pallas_call(
        _mla_lat_body,
        grid=(DEC_BATCH, n_pairs, n_qb),
        in_specs=[
            pl.BlockSpec((MLA_QB, 2 * QK_NOPE_DIM), lambda b, hp, qb: (N_CTX // MLA_QB + b * n_qb + qb, hp)),
            pl.BlockSpec((MLA_QB, LANES), lambda b, hp, qb: (N_CTX // MLA_QB + b * n_qb + qb, MLA_NOPE_W // LANES + hp)),
            pl.BlockSpec((DEC_SEQ, 2 * KV_HEAD_W), lambda b, hp, qb: (lat_blk + b, hp)),
            pl.BlockSpec((PAST_LEN, 2 * KV_HEAD_W), lambda b, hp, qb: (b, hp)),
            pl.BlockSpec((DEC_SEQ, LANES), lambda b, hp, qb: (lat_blk + b, 0)),
            pl.BlockSpec((None, PAST_LEN, LANES), lambda b, hp, qb: (b, 0, 0)),
        ],
        out_specs=pl.BlockSpec((MLA_QB, 2 * V_HEAD_DIM), lambda b, hp, qb: (b * n_qb + qb, hp)),
        out_shape=jax.ShapeDtypeStruct((N_LAT, MLA_HEADS * V_HEAD_DIM), BF16),
        compiler_params=_params(3),
        name="mla_attn_latent",
    )(q, q, kvx, kvx_cache, kpe, kpe_cache)


def _rope_tables():
    t = jnp.arange(DEC_SEQ, dtype=jnp.int32)
    rows, cols = t // GRID_W, t % GRID_W
    half = WIN_HEAD_DIM // 4
    freqs = ROPE_BASE ** (-jnp.arange(half, dtype=F32) / half)
    ang_r = rows.astype(F32)[:, None] * freqs[None, :]
    ang_c = cols.astype(F32)[:, None] * freqs[None, :]
    cr, sr, cc, sc = jnp.cos(ang_r), jnp.sin(ang_r), jnp.cos(ang_c), jnp.sin(ang_c)
    c64 = jnp.concatenate([cr, cr, cc, cc], axis=-1)
    s64 = jnp.concatenate([-sr, sr, -sc, sc], axis=-1)
    c_lat = jnp.tile(c64, (DEC_BATCH, LANES // WIN_HEAD_DIM))
    s_lat = jnp.tile(s64, (DEC_BATCH, LANES // WIN_HEAD_DIM))
    cos_t = jnp.concatenate([jnp.ones((N_CTX, LANES), F32), c_lat], axis=0)
    sin_t = jnp.concatenate([jnp.zeros((N_CTX, LANES), F32), s_lat], axis=0)
    return cos_t, sin_t


def kernel(x_prompt, x_sample, cache_win_k, cache_win_v, cache_mla_ckv, cache_mla_kpe, c, c_ctx, ada_w, ada_b, norm_mix, norm_ffn, win_w_qkv, win_w_o, win_sink, mla_w_in, mla_q_norm, mla_w_q_b, mla_kv_norm, mla_w_kv_b, mla_w_o, ffn_w_gate, ffn_w_up, ffn_w_down, norm_final):
    x = jnp.concatenate([x_prompt.reshape(N_CTX, D_MODEL), x_sample.reshape(N_LAT, D_MODEL)], axis=0)
    cond = jnp.concatenate([c_ctx[None, :], c, jnp.zeros((MOD_ROWS - 1 - DEC_BATCH, D_MODEL), F32)], axis=0)
    mod = _ada_call(cond, ada_w, ada_b).reshape(DEPTH * MOD_ROWS, 1, 6 * D_MODEL)
    tables = _rope_tables()
    cache_k = cache_win_k.reshape(DEC_BATCH, N_WIN_LAYERS, PAST_LEN, WIN_KD)
    cache_v = cache_win_v.reshape(DEC_BATCH, N_WIN_LAYERS, PAST_LEN, WIN_KD)

    win_k_out, win_v_out, ckv_out, kpe_out = [], [], [], []
    mix_gains = norm_mix.reshape(DEPTH, 1, D_MODEL)
    ffn_gains = norm_ffn.reshape(DEPTH, 1, D_MODEL)
    win_row_perm = [8 * p + 4 * s + t for p in range(WIN_KV_HEADS // 2) for t in range(WIN_GROUP) for s in range(2)]
    h = _modulate_call(x, norm_mix, 0, mod, 0, 1)
    for layer in range(DEPTH):
        j = layer // 2
        if layer % 2 == 0:
            q = _proj_call(h, win_w_qkv, j, 0, WIN_QD, BF16, tm=1024, tn=1024,
                           rope=(0, WIN_QD // 1024), tables=tables, pair_order=True, name="win_q_proj")
            kv = _proj_call(h, win_w_qkv, j, WIN_QD, 2 * WIN_KD, F32, tm=1024, tn=512,
                            rope=(0, 1), tables=tables, name="win_kv_proj")
            o_ctx = _win_ctx_call(q, kv, win_sink, j)
            o_lat = _win_lat_call(q, kv, cache_k, cache_v, win_sink, j)
            w_o, row_perm = win_w_o, win_row_perm
            win_k_out.append(kv[:N_CTX, :WIN_KD].reshape(BATCH, SEQ, WIN_KV_HEADS, WIN_HEAD_DIM))
            win_v_out.append(kv[:N_CTX, WIN_KD:].reshape(BATCH, SEQ, WIN_KV_HEADS, WIN_HEAD_DIM))
        else:
            cq, ckv, kpe = _mla_in_call(h, mla_w_in, j, mla_q_norm[j], mla_kv_norm[j], tables)
            wq = mla_w_q_b[j].reshape(Q_LORA_RANK, MLA_HEADS, QK_NOPE_DIM + QK_ROPE_DIM)
            wq = jnp.concatenate([wq[:, :, :QK_NOPE_DIM].reshape(Q_LORA_RANK, MLA_NOPE_W),
                                  wq[:, :, QK_NOPE_DIM:].reshape(Q_LORA_RANK, MLA_ROPE_W)], axis=1)
            q = _proj_call(cq, wq, 0, 0, MLA_NOPE_W + MLA_ROPE_W, BF16, tm=1024, tn=1024,
                           rope=(MLA_NOPE_W // 1024, (MLA_NOPE_W + MLA_ROPE_W) // 1024), tables=tables,
                           name="mla_q_proj")
            kvx = _proj_call(ckv, mla_w_kv_b, j, 0, MLA_HEADS * KV_HEAD_W, BF16, tm=1024, tn=1024,
                             name="mla_kv_expand")
            cache_ckv = cache_mla_ckv[:, j].reshape(DEC_BATCH * PAST_LEN, KV_LORA_RANK)
            kvx_cache = _proj_call(cache_ckv, mla_w_kv_b, j, 0, MLA_HEADS * KV_HEAD_W, BF16, tm=512, tn=1024,
                                   name="mla_kv_expand_cache")
            cache_kpe = jnp.tile(cache_mla_kpe[:, j], (1, 1, LANES // QK_ROPE_DIM))
            o_ctx = _mla_ctx_call(q, kvx, kpe)
            o_lat = _mla_lat_call(q, kvx, kvx_cache, kpe, cache_kpe)
            w_o, row_perm = mla_w_o, None
            ckv_out.append(ckv[:N_CTX].reshape(BATCH, SEQ, KV_LORA_RANK))
            kpe_out.append(kpe[:N_CTX, :QK_ROPE_DIM].reshape(BATCH, SEQ, QK_ROPE_DIM))
        x, h = _resid_norm_call([o_ctx, o_lat], w_o, j, x, mod, layer, 2, ffn_gains, layer, (layer, 3, 4),
                                tm=256, row_perm=row_perm, name="mixer_out_proj")
        act, w_down = _swiglu_call(h, ffn_w_gate, ffn_w_up, ffn_w_down, layer)
        if layer + 1 < DEPTH:
            x, h = _resid_norm_call([act], w_down, 0, x, mod, layer, 5, mix_gains, layer + 1, (layer + 1, 0, 1),
                                    tm=256, name="ffn_down_proj")
        else:
            y_ctx, y_lat = _resid_norm_call([act], w_down, 0, x, mod, layer, 5,
                                            norm_final.reshape(1, 1, D_MODEL), 0, None,
                                            tm=256, name="ffn_down_final")

    y_prompt = y_ctx.reshape(BATCH, SEQ, D_MODEL)
    y_sample = y_lat.reshape(DEC_BATCH, DEC_SEQ, D_MODEL)
    return (y_prompt, y_sample,
            jnp.stack(win_k_out, axis=1), jnp.stack(win_v_out, axis=1),
            jnp.stack(ckv_out, axis=1), jnp.stack(kpe_out, axis=1))
```

```python
import functools
import math

import numpy as np
import jax
import jax.numpy as jnp
from jax import lax
from jax.experimental import pallas as pl
from jax.experimental.pallas import tpu as pltpu

F32 = jnp.float32
BF16 = jnp.bfloat16

D_MODEL = 2048
BATCH = 16
SEQ = 256
DEPTH = 4
DEC_BATCH = 2
DEC_SEQ = 2048
PAST_LEN = 256
GRID_W = 64
N_WIN_LAYERS = 2
N_MLA_LAYERS = 2
WIN_HEADS = 32
WIN_KV_HEADS = 8
WIN_GROUP = WIN_HEADS // WIN_KV_HEADS
WIN_HEAD_DIM = 64
WINDOW = 128
BLOCK = 128
WIN_SCALE = WIN_HEAD_DIM ** -0.5
MLA_HEADS = 16
Q_LORA_RANK = 512
KV_LORA_RANK = 512
QK_NOPE_DIM = 128
QK_ROPE_DIM = 64
V_HEAD_DIM = 128
MLA_SCALE = (QK_NOPE_DIM + QK_ROPE_DIM) ** -0.5
D_FF = 5632
ROPE_BASE = 10000.0
EPS = 1e-6
NEG = float(np.finfo(np.float32).min)
LOG2E = math.log2(math.e)

N_CTX = BATCH * SEQ
N_LAT = DEC_BATCH * DEC_SEQ
M_ALL = N_CTX + N_LAT
MOD_ROWS = 16
LANES = 128
HALF_LANES = LANES // 2
WIN_QD = WIN_HEADS * WIN_HEAD_DIM
WIN_KD = WIN_KV_HEADS * WIN_HEAD_DIM
VMEM_LIMIT = 56 * 1024 * 1024
VMEM_LIMIT_LARGE = 60 * 1024 * 1024


def _params(n_axes, vmem=VMEM_LIMIT):
    return pltpu.CompilerParams(dimension_semantics=("arbitrary",) * n_axes,
                                vmem_limit_bytes=vmem)


def _group(i, tm):
    start = i * tm
    return jnp.where(start < N_CTX, 0, 1 + (start - N_CTX) // DEC_SEQ)


def _mod_spec(layer, chunk, tm, tn, m_axis, n_axis):
    nb = D_MODEL // tn

    def imap(*ids):
        j = 0 if n_axis is None else ids[n_axis]
        return (layer * MOD_ROWS + _group(ids[m_axis], tm), 0, chunk * nb + j)

    return pl.BlockSpec((None, 1, tn), imap)


def _rms(x, g):
    y = x * lax.rsqrt(jnp.mean(x * x, axis=-1, keepdims=True) + EPS)
    return y * g


def _dot(a, b):
    return jnp.dot(a, b, preferred_element_type=F32)


def _dot_nt(a, b):
    return lax.dot_general(a, b, (((1,), (1,)), ((), ())), preferred_element_type=F32)


def _rope(x, c, s):
    lane = lax.broadcasted_iota(jnp.int32, (x.shape[0], LANES), 1)
    first = (lane & 31) < 16
    outs = []
    for k in range(x.shape[1] // LANES):
        a = x[:, k * LANES:(k + 1) * LANES]
        partner = jnp.where(first, pltpu.roll(a, LANES - 16, 1), pltpu.roll(a, 16, 1))
        outs.append(a * c + partner * s)
    return outs[0] if len(outs) == 1 else jnp.concatenate(outs, axis=1)


def _ada_body(cond_ref, w_ref, b_ref, o_ref):
    s = jax.nn.silu(cond_ref[...])
    o_ref[...] = _dot(s.astype(BF16), w_ref[...].astype(BF16)) + b_ref[...]


def _ada_call(cond, ada_w, ada_b):
    tn = 1024
    n = 6 * D_MODEL
    return pl.pallas_call(
        _ada_body,
        grid=(DEPTH, n // tn),
        in_specs=[
            pl.BlockSpec((MOD_ROWS, D_MODEL), lambda l, j: (0, 0)),
            pl.BlockSpec((None, D_MODEL, tn), lambda l, j: (l, 0, j)),
            pl.BlockSpec((None, 1, tn), lambda l, j: (l, 0, j)),
        ],
        out_specs=pl.BlockSpec((None, MOD_ROWS, tn), lambda l, j: (l, 0, j)),
        out_shape=jax.ShapeDtypeStruct((DEPTH, MOD_ROWS, n), F32),
        compiler_params=_params(2),
        name="ada_modulation",
    )(cond, ada_w, ada_b.reshape(DEPTH, 1, n))


def _part_specs(n, tm, width):
    ctx_tiles = N_CTX // tm
    if n == 1:
        return [pl.BlockSpec((tm, width), lambda i: (i, 0))]
    return [pl.BlockSpec((tm, width), lambda i: (jnp.minimum(i, ctx_tiles - 1), 0)),
            pl.BlockSpec((tm, width), lambda i: (jnp.maximum(i - ctx_tiles, 0), 0))]


def _modulate_body(xc_ref, xl_ref, g_ref, sc_ref, sh_ref, o_ref, *, ctx_tiles):
    def run(x_ref):
        y = _rms(x_ref[...], g_ref[...])
        o_ref[...] = (y * (1.0 + sc_ref[...]) + sh_ref[...]).astype(o_ref.dtype)

    i = pl.program_id(0)
    pl.when(i < ctx_tiles)(lambda: run(xc_ref))
    pl.when(i >= ctx_tiles)(lambda: run(xl_ref))


def _modulate_call(x_ctx, x_lat, gains, layer, mod, shift_chunk, scale_chunk):
    tm = 512
    return pl.pallas_call(
        functools.partial(_modulate_body, ctx_tiles=N_CTX // tm),
        grid=(M_ALL // tm,),
        in_specs=_part_specs(2, tm, D_MODEL) + [
            pl.BlockSpec((None, 1, D_MODEL), lambda i: (layer, 0, 0)),
            _mod_spec(layer, scale_chunk, tm, D_MODEL, 0, None),
            _mod_spec(layer, shift_chunk, tm, D_MODEL, 0, None),
        ],
        out_specs=pl.BlockSpec((tm, D_MODEL), lambda i: (i, 0)),
        out_shape=jax.ShapeDtypeStruct((M_ALL, D_MODEL), BF16),
        compiler_params=_params(1),
        name="norm_modulate",
    )(x_ctx, x_lat, gains.reshape(DEPTH, 1, D_MODEL), mod, mod)


def _low_lanes(rows):
    return lax.broadcasted_iota(jnp.int32, (rows, LANES), 1) < HALF_LANES


def _cast_pair_order(w_ref, w_s):
    low = _low_lanes(w_ref.shape[0])
    per_pair = 2 * WIN_GROUP * WIN_HEAD_DIM // LANES
    for p in range(w_ref.shape[1] // (per_pair * LANES)):
        c = [w_ref[:, (per_pair * p + t) * LANES:(per_pair * p + t + 1) * LANES] for t in range(per_pair)]
        r = [pltpu.roll(x, HALF_LANES, 1) for x in c]
        moved = [jnp.where(low, c[0], r[2]), jnp.where(low, r[0], c[2]),
                 jnp.where(low, c[1], r[3]), jnp.where(low, r[1], c[3])]
        for t, x in enumerate(moved):
            w_s[:, (per_pair * p + t) * LANES:(per_pair * p + t + 1) * LANES] = x.astype(BF16)


def _proj_body(*refs, rope_lo, rope_hi, n_ctx_tiles, pair_order):
    has_rope = rope_hi > rope_lo
    if has_rope:
        lhs_ref, w_ref, cos_ref, sin_ref, o_ref, w_s = refs
    else:
        lhs_ref, w_ref, o_ref, w_s = refs
    j = pl.program_id(0)
    i = pl.program_id(1)

    @pl.when(i == 0)
    def _():
        if pair_order:
            _cast_pair_order(w_ref, w_s)
        else:
            w_s[...] = w_ref[...].astype(BF16)

    acc = _dot(lhs_ref[...].astype(BF16), w_s[...])
    if not has_rope:
        o_ref[...] = acc.astype(o_ref.dtype)
    else:
        roped = jnp.logical_and(jnp.logical_and(j >= rope_lo, j < rope_hi), i >= n_ctx_tiles)

        @pl.when(roped)
        def _():
            o_ref[...] = _rope(acc, cos_ref[...], sin_ref[...]).astype(o_ref.dtype)

        @pl.when(jnp.logical_not(roped))
        def _():
            o_ref[...] = acc.astype(o_ref.dtype)


def _proj_call(lhs, w, layer, col_off, n_cols, out_dtype, *, tm, tn, rope=None, tables=None,
               pair_order=False, name):
    m, k = lhs.shape
    n_tiles = n_cols // tn
    off = col_off // tn
    rope_lo, rope_hi = rope if rope is not None else (0, 0)
    if w.ndim == 3:
        w_spec = pl.BlockSpec((None, k, tn), lambda j, i: (layer, 0, off + j))
    else:
        w_spec = pl.BlockSpec((k, tn), lambda j, i: (0, off + j))
    in_specs = [pl.BlockSpec((tm, k), lambda j, i: (i, 0)), w_spec]
    args = [lhs, w]
    if rope_hi > rope_lo:
        in_specs += [pl.BlockSpec((tm, LANES), lambda j, i: (i, 0))] * 2
        args += list(tables)
    return pl.pallas_call(
        functools.partial(_proj_body, rope_lo=rope_lo, rope_hi=rope_hi, n_ctx_tiles=N_CTX // tm,
                          pair_order=pair_order),
        grid=(n_tiles, m // tm),
        in_specs=in_specs,
        out_specs=pl.BlockSpec((tm, tn), lambda j, i: (i, j)),
        out_shape=jax.ShapeDtypeStruct((m, n_cols), out_dtype),
        scratch_shapes=[pltpu.VMEM((k, tn), BF16)],
        compiler_params=_params(2),
        name=name,
    )(*args)


ROW_BLOCK = WIN_HEAD_DIM

def _resid_norm_body(*refs, n_parts, n_x, ctx_tiles, cast_w, row_perm, final):
    lhs_refs = refs[:n_parts]
    w_ref = refs[n_parts]
    x_refs = refs[n_parts + 1:n_parts + 1 + n_x]
    n_mod = 2 if final else 4
    mods = refs[n_parts + 1 + n_x:n_parts + 1 + n_x + n_mod]
    outs = refs[n_parts + 1 + n_x + n_mod:n_parts + 1 + n_x + n_mod + 2]
    scratch = refs[n_parts + 1 + n_x + n_mod + 2:]
    gate_ref, g_ref = mods[:2]
    i = pl.program_id(0)
    if cast_w:
        w = scratch[0]

        @pl.when(i == 0)
        def _():
            if row_perm is None:
                w[...] = w_ref[...].astype(BF16)
            else:
                for new, old in enumerate(row_perm):
                    w[new * ROW_BLOCK:(new + 1) * ROW_BLOCK, :] = (
                        w_ref[old * ROW_BLOCK:(old + 1) * ROW_BLOCK, :].astype(BF16))
    else:
        w = w_ref

    def step(lhs_ref, x_ref, y_ref):
        xn = x_ref[...] + gate_ref[...] * _dot(lhs_ref[...], w[...])
        y = _rms(xn, g_ref[...])
        if final:
            y_ref[...] = y
        else:
            outs[0][...] = xn
            outs[1][...] = (y * (1.0 + mods[2][...]) + mods[3][...]).astype(outs[1].dtype)

    if n_parts == 1 and n_x == 1 and not final:
        step(lhs_refs[0], x_refs[0], None)
    else:
        pl.when(i < ctx_tiles)(lambda: step(lhs_refs[0], x_refs[0], outs[0]))
        pl.when(i >= ctx_tiles)(lambda: step(lhs_refs[-1], x_refs[-1], outs[1]))


def _resid_norm_call(lhs_parts, w, w_layer, x_parts, mod, layer, gate_chunk, gains, gain_layer, next_mod, *,
                     tm, row_perm=None, name):
    k = lhs_parts[0].shape[1]
    n_parts = len(lhs_parts)
    n_x = len(x_parts)
    ctx_tiles = N_CTX // tm
    final = next_mod is None
    cast_w = w.dtype != BF16
    if w.ndim == 3:
        w_spec = pl.BlockSpec((None, k, D_MODEL), lambda i: (w_layer, 0, 0), pipeline_mode=pl.Buffered(1))
    else:
        w_spec = pl.BlockSpec((k, D_MODEL), lambda i: (0, 0))
    in_specs = _part_specs(n_parts, tm, k) + [w_spec] + _part_specs(n_x, tm, D_MODEL) + [
        _mod_spec(layer, gate_chunk, tm, D_MODEL, 0, None),
        pl.BlockSpec((None, 1, D_MODEL), lambda i: (gain_layer, 0, 0)),
    ]
    args = list(lhs_parts) + [w] + list(x_parts) + [mod, gains]
    if final:
        out_specs = _part_specs(2, tm, D_MODEL)
        out_shape = [jax.ShapeDtypeStruct((N_CTX, D_MODEL), F32), jax.ShapeDtypeStruct((N_LAT, D_MODEL), F32)]
    else:
        nl, shift_chunk, scale_chunk = next_mod
        in_specs += [_mod_spec(nl, scale_chunk, tm, D_MODEL, 0, None),
                     _mod_spec(nl, shift_chunk, tm, D_MODEL, 0, None)]
        args += [mod, mod]
        out_specs = _part_specs(1, tm, D_MODEL) * 2
        out_shape = [jax.ShapeDtypeStruct((M_ALL, D_MODEL), F32), jax.ShapeDtypeStruct((M_ALL, D_MODEL), BF16)]
    return pl.pallas_call(
        functools.partial(_resid_norm_body, n_parts=n_parts, n_x=n_x, ctx_tiles=ctx_tiles, cast_w=cast_w,
                          row_perm=row_perm, final=final),
        grid=(M_ALL // tm,),
        in_specs=in_specs,
        out_specs=out_specs,
        out_shape=out_shape,
        scratch_shapes=[pltpu.VMEM((k, D_MODEL), BF16)] if cast_w else [],
        compiler_params=_params(1),
        name=name,
    )(*args)


def _swiglu_body(h_ref, wg_ref, wu_ref, wd_ref, o_ref, wdo_ref, wg_s, wu_s):
    @pl.when(pl.program_id(1) == 0)
    def _():
        wg_s[...] = wg_ref[...].astype(BF16)
        wu_s[...] = wu_ref[...].astype(BF16)

    h = h_ref[...]
    g = _dot(h, wg_s[...])
    u = _dot(h, wu_s[...])
    o_ref[...] = (jax.nn.silu(g) * u).astype(o_ref.dtype)
    wdo_ref[...] = wd_ref[...].astype(BF16)


def _swiglu_call(h, w_gate, w_up, w_down, layer):
    tm, tn = 2048, 512
    n_i = M_ALL // tm
    slab = D_FF // ((D_FF // tn) * n_i)
    return pl.pallas_call(
        _swiglu_body,
        grid=(D_FF // tn, n_i),
        in_specs=[
            pl.BlockSpec((tm, D_MODEL), lambda j, i: (i, 0)),
            pl.BlockSpec((None, D_MODEL, tn), lambda j, i: (layer, 0, j)),
            pl.BlockSpec((None, D_MODEL, tn), lambda j, i: (layer, 0, j)),
            pl.BlockSpec((None, slab, D_MODEL), lambda j, i: (layer, j * n_i + i, 0)),
        ],
        out_specs=[
            pl.BlockSpec((tm, tn), lambda j, i: (i, j)),
            pl.BlockSpec((slab, D_MODEL), lambda j, i: (j * n_i + i, 0)),
        ],
        out_shape=[jax.ShapeDtypeStruct((M_ALL, D_FF), BF16), jax.ShapeDtypeStruct((D_FF, D_MODEL), BF16)],
        scratch_shapes=[pltpu.VMEM((D_MODEL, tn), BF16), pltpu.VMEM((D_MODEL, tn), BF16)],
        compiler_params=_params(2, vmem=VMEM_LIMIT_LARGE),
        name="ffn_gate_up",
    )(h, w_gate, w_up, w_down)


def _mla_in_body(h_ref, w_ref, qn_ref, kvn_ref, cos_ref, sin_ref, cq_ref, ckv_ref, kpe_ref, w_s):
    @pl.when(pl.program_id(0) == 0)
    def _():
        w_s[...] = w_ref[...].astype(BF16)

    acc = _dot(h_ref[...], w_s[...])
    cq = acc[:, :Q_LORA_RANK]
    ckv = acc[:, Q_LORA_RANK:Q_LORA_RANK + KV_LORA_RANK]
    kpe = acc[:, Q_LORA_RANK + KV_LORA_RANK:]
    kpe = jnp.concatenate([kpe, kpe], axis=1)
    cq_ref[...] = _rms(cq, qn_ref[...]).astype(cq_ref.dtype)
    ckv_ref[...] = _rms(ckv, kvn_ref[...])
    kpe_ref[...] = _rope(kpe, cos_ref[...], sin_ref[...])


def _mla_in_call(h, w_in, layer, q_norm, kv_norm, tables):
    tm = 1024
    row = lambda i: (i, 0)
    fixed = lambda i: (0, 0)
    n_in = Q_LORA_RANK + KV_LORA_RANK + QK_ROPE_DIM
    return pl.pallas_call(
        _mla_in_body,
        grid=(M_ALL // tm,),
        in_specs=[
            pl.BlockSpec((tm, D_MODEL), row),
            pl.BlockSpec((None, D_MODEL, n_in), lambda i: (layer, 0, 0), pipeline_mode=pl.Buffered(1)),
            pl.BlockSpec((1, Q_LORA_RANK), fixed),
            pl.BlockSpec((1, KV_LORA_RANK), fixed),
            pl.BlockSpec((tm, LANES), row),
            pl.BlockSpec((tm, LANES), row),
        ],
        out_specs=[
            pl.BlockSpec((tm, Q_LORA_RANK), row),
            pl.BlockSpec((tm, KV_LORA_RANK), row),
            pl.BlockSpec((tm, LANES), row),
        ],
        out_shape=[
            jax.ShapeDtypeStruct((M_ALL, Q_LORA_RANK), BF16),
            jax.ShapeDtypeStruct((M_ALL, KV_LORA_RANK), F32),
            jax.ShapeDtypeStruct((M_ALL, LANES), F32),
        ],
        scratch_shapes=[pltpu.VMEM((D_MODEL, n_in), BF16)],
        compiler_params=_params(1),
        name="mla_in_proj",
    )(h, w_in, q_norm.reshape(1, -1), kv_norm.reshape(1, -1), *tables)


def _win_heads(q_ref, o_ref, sink_ref, layer, k_chunks, v_chunks, blocks, nq, pipelined):
    low_q = _low_lanes(nq)
    qlo = jnp.where(low_q, WIN_SCALE, 0.0).astype(BF16)
    qhi = jnp.where(low_q, 0.0, WIN_SCALE).astype(BF16)
    heads_per_pair = 2 * WIN_GROUP

    def scores(p):
        kc = k_chunks[p].astype(BF16)
        qcs = [q_ref[:, (WIN_GROUP * p + t) * LANES:(WIN_GROUP * p + t + 1) * LANES] for t in range(WIN_GROUP)]
        qs = jnp.concatenate([qc * qlo for qc in qcs] + [qc * qhi for qc in qcs], axis=0)
        return _dot_nt(qs, kc)

    def softmax(p, lg):
        probs, invs = [], []
        for u in range(heads_per_pair):
            sk = sink_ref[layer, heads_per_pair * p + u]
            l = lg[u * nq:(u + 1) * nq]
            cols = [l[:, a:a + LANES] if msk is None else jnp.where(msk, l[:, a:a + LANES], NEG)
                    for a, msk in blocks]
            mx = cols[0]
            for col in cols[1:]:
                mx = jnp.maximum(mx, col)
            m = jnp.maximum(jnp.max(mx, axis=-1, keepdims=True), sk)
            es = [jnp.exp(col - m) for col in cols]
            tot = es[0]
            for e in es[1:]:
                tot = tot + e
            invs.append(1.0 / (jnp.sum(tot, axis=-1, keepdims=True) + jnp.exp(sk - m)))
            probs.append(jnp.concatenate([e.astype(BF16) for e in es], axis=1))
        return jnp.concatenate(probs, axis=0), invs

    def values(p, probs, invs):
        o = _dot(probs, v_chunks[p].astype(BF16))
        on = [o[u * nq:(u + 1) * nq] * invs[u] for u in range(heads_per_pair)]
        for t in range(WIN_GROUP):
            c = WIN_GROUP * p + t
            o_ref[:, c * LANES:(c + 1) * LANES] = jnp.where(low_q, on[t], on[WIN_GROUP + t]).astype(o_ref.dtype)

    n = WIN_KV_HEADS // 2
    if not pipelined:
        for p in range(n):
            values(p, *softmax(p, scores(p)))
        return
    sc, pr = {}, {}
    for p in range(n + 2):
        if p < n:
            sc[p] = scores(p)
        if 0 <= p - 1 < n:
            pr[p - 1] = softmax(p - 1, sc.pop(p - 1))
        if 0 <= p - 2 < n:
            values(p - 2, *pr.pop(p - 2))


def _win_ctx_body(sink_ref, q_ref, kv_ref, o_ref, *, layer):
    n_pairs = WIN_KD // LANES
    k_chunks = [kv_ref[:, p * LANES:(p + 1) * LANES] for p in range(n_pairs)]
    v_chunks = [kv_ref[:, WIN_KD + p * LANES:WIN_KD + (p + 1) * LANES] for p in range(n_pairs)]
    _win_heads(q_ref, o_ref, sink_ref, layer, k_chunks, v_chunks,
               [(a, None) for a in range(0, SEQ, LANES)], SEQ, False)


def _win_ctx_call(q, kv, sink, layer):
    return pl.pallas_call(
        functools.partial(_win_ctx_body, layer=layer),
        grid=(BATCH,),
        in_specs=[
            pl.BlockSpec(memory_space=pltpu.SMEM),
            pl.BlockSpec((SEQ, WIN_QD), lambda b: (b, 0)),
            pl.BlockSpec((SEQ, 2 * WIN_KD), lambda b: (b, 0)),
        ],
        out_specs=pl.BlockSpec((SEQ, WIN_QD), lambda b: (b, 0)),
        out_shape=jax.ShapeDtypeStruct((N_CTX, WIN_QD), BF16),
        compiler_params=_params(1),
        name="win_attn_context",
    )(sink, q, kv)


def _win_lat_body(sink_ref, q_ref, kvp_ref, kvc_ref, kvn_ref, ck_ref, cv_ref, o_ref, *, layer):
    n = pl.program_id(1)
    n_pairs = WIN_KD // LANES
    k_chunks, v_chunks = [], []
    for p in range(n_pairs):
        ks = slice(p * LANES, (p + 1) * LANES)
        vs = slice(WIN_KD + p * LANES, WIN_KD + (p + 1) * LANES)
        k_chunks.append(jnp.concatenate([kvp_ref[:, ks], kvc_ref[:, ks], kvn_ref[:, ks], ck_ref[:, ks]], axis=0))
        v_chunks.append(jnp.concatenate([kvp_ref[:, vs], kvc_ref[:, vs], kvn_ref[:, vs], cv_ref[:, ks]], axis=0))
    r = lax.broadcasted_iota(jnp.int32, (BLOCK, BLOCK), 0)
    s = lax.broadcasted_iota(jnp.int32, (BLOCK, BLOCK), 1)
    prev_ok = (s - r) >= jnp.where(n >= 1, 0, BLOCK)
    next_ok = (r - s) >= jnp.where(n <= DEC_SEQ // BLOCK - 2, 0, BLOCK)
    blocks = [(0, prev_ok), (BLOCK, None), (2 * BLOCK, next_ok)]
    blocks += [(3 * BLOCK + a, None) for a in range(0, PAST_LEN, LANES)]
    _win_heads(q_ref, o_ref, sink_ref, layer, k_chunks, v_chunks, blocks, BLOCK, True)


def _win_lat_call(q, kv, cache_k, cache_v, sink, layer):
    nb = DEC_SEQ // BLOCK
    base = N_CTX // BLOCK

    def rows(shift):
        return lambda b, n: (base + b * nb + jnp.clip(n + shift, 0, nb - 1), 0)

    cache_spec = pl.BlockSpec((None, None, PAST_LEN, WIN_KD), lambda b, n: (b, layer, 0, 0))
    return pl.pallas_call(
        functools.partial(_win_lat_body, layer=layer),
        grid=(DEC_BATCH, nb),
        in_specs=[
            pl.BlockSpec(memory_space=pltpu.SMEM),
            pl.BlockSpec((BLOCK, WIN_QD), rows(0)),
            pl.BlockSpec((BLOCK, 2 * WIN_KD), rows(-1)),
            pl.BlockSpec((BLOCK, 2 * WIN_KD), rows(0)),
            pl.BlockSpec((BLOCK, 2 * WIN_KD), rows(1)),
            cache_spec,
            cache_spec,
        ],
        out_specs=pl.BlockSpec((BLOCK, WIN_QD), lambda b, n: (b * nb + n, 0)),
        out_shape=jax.ShapeDtypeStruct((N_LAT, WIN_QD), BF16),
        compiler_params=_params(2),
        name="win_attn_latent",
    )(sink, q, kv, kv, kv, cache_k, cache_v)


KV_HEAD_W = QK_NOPE_DIM + V_HEAD_DIM
MLA_NOPE_W = MLA_HEADS * QK_NOPE_DIM
MLA_ROPE_W = MLA_HEADS * QK_ROPE_DIM
MLA_EXP2_SCALE = MLA_SCALE * LOG2E


def _half_masks_bf16(rows):
    low = _low_lanes(rows)
    return jnp.where(low, 1.0, 0.0).astype(BF16), jnp.where(low, 0.0, 1.0).astype(BF16)


def _mla_ctx_body(qn_ref, qp_ref, kvx_ref, kpe_ref, o_ref):
    qlo, qhi = _half_masks_bf16(SEQ)
    kpe2 = kpe_ref[...].astype(BF16)
    for h in range(MLA_HEADS):
        qp = qp_ref[:, (h // 2) * LANES:(h // 2 + 1) * LANES] * (qlo if h % 2 == 0 else qhi)
        qcat = jnp.concatenate([qn_ref[:, h * LANES:(h + 1) * LANES], qp], axis=1)
        kcat = jnp.concatenate([kvx_ref[:, h * KV_HEAD_W:h * KV_HEAD_W + QK_NOPE_DIM], kpe2], axis=1)
        lg = _dot_nt(qcat, kcat)
        m = jnp.max(lg, axis=-1, keepdims=True)
        e = jnp.exp2((lg - m) * MLA_EXP2_SCALE)
        inv = 1.0 / jnp.sum(e, axis=-1, keepdims=True)
        o = _dot(e.astype(BF16), kvx_ref[:, h * KV_HEAD_W + QK_NOPE_DIM:(h + 1) * KV_HEAD_W])
        o_ref[:, h * LANES:(h + 1) * LANES] = (o * inv).astype(o_ref.dtype)


def _mla_ctx_call(q, kvx, kpe):
    return pl.pallas_call(
        _mla_ctx_body,
        grid=(BATCH,),
        in_specs=[
            pl.BlockSpec((SEQ, MLA_NOPE_W), lambda b: (b, 0)),
            pl.BlockSpec((SEQ, MLA_ROPE_W), lambda b: (b, MLA_NOPE_W // MLA_ROPE_W)),
            pl.BlockSpec((SEQ, MLA_HEADS * KV_HEAD_W), lambda b: (b, 0)),
            pl.BlockSpec((SEQ, LANES), lambda b: (b, 0)),
        ],
        out_specs=pl.BlockSpec((SEQ, MLA_HEADS * V_HEAD_DIM), lambda b: (b, 0)),
        out_shape=jax.ShapeDtypeStruct((N_CTX, MLA_HEADS * V_HEAD_DIM), BF16),
        compiler_params=_params(1),
        name="mla_attn_context",
    )(q, q, kvx, kpe)


MLA_QB = 2048
MLA_SUB = 512


def _mla_lat_body(qn_ref, qp_ref, kvl_ref, kvc_ref, kpl_ref, kpc_ref, o_ref):
    qlo, qhi = _half_masks_bf16(MLA_SUB)
    kpl = kpl_ref[...].astype(BF16)
    kpc = kpc_ref[...].astype(BF16)
    qp_pair = qp_ref[...]
    chains = [(t, r) for t in range(2) for r in range(MLA_QB // MLA_SUB)]

    def scores(t, r):
        c0 = t * KV_HEAD_W
        rows = slice(r * MLA_SUB, (r + 1) * MLA_SUB)
        qcat = jnp.concatenate([qn_ref[rows, t * LANES:(t + 1) * LANES],
                                qp_pair[rows] * (qlo if t == 0 else qhi)], axis=1)
        k_lat = jnp.concatenate([kvl_ref[:, c0:c0 + QK_NOPE_DIM], kpl], axis=1)
        k_cache = jnp.concatenate([kvc_ref[:, c0:c0 + QK_NOPE_DIM], kpc], axis=1)
        return _dot_nt(qcat, k_lat), _dot_nt(qcat, k_cache)

    def softmax(l1, l2):
        m = jnp.maximum(jnp.max(l1, axis=-1, keepdims=True), jnp.max(l2, axis=-1, keepdims=True))
        e1 = jnp.exp2((l1 - m) * MLA_EXP2_SCALE)
        e2 = jnp.exp2((l2 - m) * MLA_EXP2_SCALE)
        inv = 1.0 / (jnp.sum(e1, axis=-1, keepdims=True) + jnp.sum(e2, axis=-1, keepdims=True))
        return e1.astype(BF16), e2.astype(BF16), inv

    def values(t, r, p1, p2, inv):
        c0 = t * KV_HEAD_W
        o = (_dot(p1, kvl_ref[:, c0 + QK_NOPE_DIM:c0 + KV_HEAD_W])
             + _dot(p2, kvc_ref[:, c0 + QK_NOPE_DIM:c0 + KV_HEAD_W]))
        o_ref[r * MLA_SUB:(r + 1) * MLA_SUB, t * LANES:(t + 1) * LANES] = (o * inv).astype(o_ref.dtype)

    n = len(chains)
    sc, pr = {}, {}
    for c in range(n + 2):
        if c < n:
            sc[c] = scores(*chains[c])
        if 0 <= c - 1 < n:
            pr[c - 1] = softmax(*sc.pop(c - 1))
        if 0 <= c - 2 < n:
            values(*chains[c - 2], *pr.pop(c - 2))


def _mla_lat_call(q, kvx, kvx_cache, kpe, kpe_cache):
    n_qb = DEC_SEQ // MLA_QB
    n_pairs = MLA_HEADS // 2
    lat_blk = N_CTX // DEC_SEQ
    return pl.pallas_call(
        _mla_lat_body,
        grid=(DEC_BATCH, n_pairs, n_qb),
        in_specs=[
            pl.BlockSpec((MLA_QB, 2 * QK_NOPE_DIM), lambda b, hp, qb: (N_CTX // MLA_QB + b * n_qb + qb, hp)),
            pl.BlockSpec((MLA_QB, LANES), lambda b, hp, qb: (N_CTX // MLA_QB + b * n_qb + qb, MLA_NOPE_W // LANES + hp)),
            pl.BlockSpec((DEC_SEQ, 2 * KV_HEAD_W), lambda b, hp, qb: (lat_blk + b, hp)),
            pl.BlockSpec((PAST_LEN, 2 * KV_HEAD_W), lambda b, hp, qb: (b, hp)),
            pl.BlockSpec((DEC_SEQ, LANES), lambda b, hp, qb: (lat_blk + b, 0)),
            pl.BlockSpec((None, PAST_LEN, LANES), lambda b, hp, qb: (b, 0, 0)),
        ],
        out_specs=pl.BlockSpec((MLA_QB, 2 * V_HEAD_DIM), lambda b, hp, qb: (b * n_qb + qb, hp)),
        out_shape=jax.ShapeDtypeStruct((N_LAT, MLA_HEADS * V_HEAD_DIM), BF16),
        compiler_params=_params(3),
        name="mla_attn_latent",
    )(q, q, kvx, kvx_cache, kpe, kpe_cache)


def _pack_body(*refs, n_layers, col_ranges):
    ins, outs = refs[:-len(col_ranges)], refs[-len(col_ranges):]
    for o_ref, (src, lo, hi) in zip(outs, col_ranges):
        for l in range(n_layers):
            o_ref[0, l] = ins[src * n_layers + l][:, lo:hi]


def _pack_call(sources, col_ranges, name):
    n_layers = len(sources[0])
    flat = [a for src in sources for a in src]
    return pl.pallas_call(
        functools.partial(_pack_body, n_layers=n_layers, col_ranges=col_ranges),
        grid=(BATCH,),
        in_specs=[pl.BlockSpec((SEQ, a.shape[1]), lambda b: (b, 0)) for a in flat],
        out_specs=[pl.BlockSpec((1, n_layers, SEQ, hi - lo), lambda b: (b, 0, 0, 0)) for _, lo, hi in col_ranges],
        out_shape=[jax.ShapeDtypeStruct((BATCH, n_layers, SEQ, hi - lo), F32) for _, lo, hi in col_ranges],
        compiler_params=_params(1),
        name=name,
    )(*flat)


def _rope_tables():
    t = jnp.arange(DEC_SEQ, dtype=jnp.int32)
    rows, cols = t // GRID_W, t % GRID_W
    half = WIN_HEAD_DIM // 4
    freqs = ROPE_BASE ** (-jnp.arange(half, dtype=F32) / half)
    ang_r = rows.astype(F32)[:, None] * freqs[None, :]
    ang_c = cols.astype(F32)[:, None] * freqs[None, :]
    cr, sr, cc, sc = jnp.cos(ang_r), jnp.sin(ang_r), jnp.cos(ang_c), jnp.sin(ang_c)
    c64 = jnp.concatenate([cr, cr, cc, cc], axis=-1)
    s64 = jnp.concatenate([-sr, sr, -sc, sc], axis=-1)
    c_lat = jnp.tile(c64, (DEC_BATCH, LANES // WIN_HEAD_DIM))
    s_lat = jnp.tile(s64, (DEC_BATCH, LANES // WIN_HEAD_DIM))
    cos_t = jnp.concatenate([jnp.ones((N_CTX, LANES), F32), c_lat], axis=0)
    sin_t = jnp.concatenate([jnp.zeros((N_CTX, LANES), F32), s_lat], axis=0)
    return cos_t, sin_t


def kernel(x_prompt, x_sample, cache_win_k, cache_win_v, cache_mla_ckv, cache_mla_kpe, c, c_ctx, ada_w, ada_b, norm_mix, norm_ffn, win_w_qkv, win_w_o, win_sink, mla_w_in, mla_q_norm, mla_w_q_b, mla_kv_norm, mla_w_kv_b, mla_w_o, ffn_w_gate, ffn_w_up, ffn_w_down, norm_final):
    x = [x_prompt.reshape(N_CTX, D_MODEL), x_sample.reshape(N_LAT, D_MODEL)]
    cond = jnp.concatenate([c_ctx[None, :], c, jnp.zeros((MOD_ROWS - 1 - DEC_BATCH, D_MODEL), F32)], axis=0)
    mod = _ada_call(cond, ada_w, ada_b).reshape(DEPTH * MOD_ROWS, 1, 6 * D_MODEL)
    tables = _rope_tables()
    cache_k = cache_win_k.reshape(DEC_BATCH, N_WIN_LAYERS, PAST_LEN, WIN_KD)
    cache_v = cache_win_v.reshape(DEC_BATCH, N_WIN_LAYERS, PAST_LEN, WIN_KD)

    kv_layers, ckv_layers, kpe_layers = [], [], []
    mix_gains = norm_mix.reshape(DEPTH, 1, D_MODEL)
    ffn_gains = norm_ffn.reshape(DEPTH, 1, D_MODEL)
    win_row_perm = [8 * p + 4 * s + t for p in range(WIN_KV_HEADS // 2) for t in range(WIN_GROUP) for s in range(2)]
    h = _modulate_call(x[0], x[1], norm_mix, 0, mod, 0, 1)
    for layer in range(DEPTH):
        j = layer // 2
        if layer % 2 == 0:
            q = _proj_call(h, win_w_qkv, j, 0, WIN_QD, BF16, tm=1024, tn=1024,
                           rope=(0, WIN_QD // 1024), tables=tables, pair_order=True, name="win_q_proj")
            kv = _proj_call(h, win_w_qkv, j, WIN_QD, 2 * WIN_KD, F32, tm=1024, tn=512,
                            rope=(0, 1), tables=tables, name="win_kv_proj")
            o_ctx = _win_ctx_call(q, kv, win_sink, j)
            o_lat = _win_lat_call(q, kv, cache_k, cache_v, win_sink, j)
            w_o, row_perm = win_w_o, win_row_perm
            kv_layers.append(kv)
        else:
            cq, ckv, kpe = _mla_in_call(h, mla_w_in, j, mla_q_norm[j], mla_kv_norm[j], tables)
            wq = mla_w_q_b[j].reshape(Q_LORA_RANK, MLA_HEADS, QK_NOPE_DIM + QK_ROPE_DIM)
            wq = jnp.concatenate([wq[:, :, :QK_NOPE_DIM].reshape(Q_LORA_RANK, MLA_NOPE_W),
                                  wq[:, :, QK_NOPE_DIM:].reshape(Q_LORA_RANK, MLA_ROPE_W)], axis=1)
            q = _proj_call(cq, wq, 0, 0, MLA_NOPE_W + MLA_ROPE_W, BF16, tm=1024, tn=1024,
                           rope=(MLA_NOPE_W // 1024, (MLA_NOPE_W + MLA_ROPE_W) // 1024), tables=tables,
                           name="mla_q_proj")
            kvx = _proj_call(ckv, mla_w_kv_b, j, 0, MLA_HEADS * KV_HEAD_W, BF16, tm=1024, tn=1024,
                             name="mla_kv_expand")
            cache_ckv = cache_mla_ckv[:, j].reshape(DEC_BATCH * PAST_LEN, KV_LORA_RANK)
            kvx_cache = _proj_call(cache_ckv, mla_w_kv_b, j, 0, MLA_HEADS * KV_HEAD_W, BF16, tm=512, tn=1024,
                                   name="mla_kv_expand_cache")
            cache_kpe = jnp.tile(cache_mla_kpe[:, j], (1, 1, LANES // QK_ROPE_DIM))
            o_ctx = _mla_ctx_call(q, kvx, kpe)
            o_lat = _mla_lat_call(q, kvx, kvx_cache, kpe, cache_kpe)
            w_o, row_perm = mla_w_o, None
            ckv_layers.append(ckv)
            kpe_layers.append(kpe)
        xn, h = _resid_norm_call([o_ctx, o_lat], w_o, j, x, mod, layer, 2, ffn_gains, layer, (layer, 3, 4),
                                 tm=256, row_perm=row_perm, name="mixer_out_proj")
        act, w_down = _swiglu_call(h, ffn_w_gate, ffn_w_up, ffn_w_down, layer)
        if layer + 1 < DEPTH:
            xn, h = _resid_norm_call([act], w_down, 0, [xn], mod, layer, 5, mix_gains, layer + 1,
                                     (layer + 1, 0, 1), tm=256, name="ffn_down_proj")
            x = [xn]
        else:
            y_ctx, y_lat = _resid_norm_call([act], w_down, 0, [xn], mod, layer, 5,
                                            norm_final.reshape(1, 1, D_MODEL), 0, None,
                                            tm=256, name="ffn_down_final")

    y_prompt = y_ctx.reshape(BATCH, SEQ, D_MODEL)
    y_sample = y_lat.reshape(DEC_BATCH, DEC_SEQ, D_MODEL)
    new_k, new_v = _pack_call([kv_layers], [(0, 0, WIN_KD), (0, WIN_KD, 2 * WIN_KD)], "pack_win_cache")
    new_ckv, new_kpe = _pack_call([ckv_layers, kpe_layers], [(0, 0, KV_LORA_RANK), (1, 0, QK_ROPE_DIM)],
                                  "pack_mla_cache")
    head_shape = (BATCH, N_WIN_LAYERS, SEQ, WIN_KV_HEADS, WIN_HEAD_DIM)
    return (y_prompt, y_sample, new_k.reshape(head_shape), new_v.reshape(head_shape), new_ckv, new_kpe)
```

```python
import functools
import math

import numpy as np
import jax
import jax.numpy as jnp
from jax import lax
from jax.experimental import pallas as pl
from jax.experimental.pallas import tpu as pltpu

F32 = jnp.float32
BF16 = jnp.bfloat16

D_MODEL = 2048
BATCH = 16
SEQ = 256
DEPTH = 4
DEC_BATCH = 2
DEC_SEQ = 2048
PAST_LEN = 256
GRID_W = 64
N_WIN_LAYERS = 2
N_MLA_LAYERS = 2
WIN_HEADS = 32
WIN_KV_HEADS = 8
WIN_GROUP = WIN_HEADS // WIN_KV_HEADS
WIN_HEAD_DIM = 64
WINDOW = 128
BLOCK = 128
WIN_SCALE = WIN_HEAD_DIM ** -0.5
MLA_HEADS = 16
Q_LORA_RANK = 512
KV_LORA_RANK = 512
QK_NOPE_DIM = 128
QK_ROPE_DIM = 64
V_HEAD_DIM = 128
MLA_SCALE = (QK_NOPE_DIM + QK_ROPE_DIM) ** -0.5
D_FF = 5632
ROPE_BASE = 10000.0
EPS = 1e-6
NEG = float(np.finfo(np.float32).min)
LOG2E = math.log2(math.e)

N_CTX = BATCH * SEQ
N_LAT = DEC_BATCH * DEC_SEQ
M_ALL = N_CTX + N_LAT
MOD_ROWS = 16
LANES = 128
HALF_LANES = LANES // 2
WIN_QD = WIN_HEADS * WIN_HEAD_DIM
WIN_KD = WIN_KV_HEADS * WIN_HEAD_DIM
VMEM_LIMIT = 56 * 1024 * 1024
VMEM_LIMIT_LARGE = 60 * 1024 * 1024


def _params(n_axes, vmem=VMEM_LIMIT):
    return pltpu.CompilerParams(dimension_semantics=("arbitrary",) * n_axes,
                                vmem_limit_bytes=vmem)


def _group(i, tm):
    start = i * tm
    return jnp.where(start < N_CTX, 0, 1 + (start - N_CTX) // DEC_SEQ)


def _mod_spec(layer, chunk, tm, tn, m_axis, n_axis):
    nb = D_MODEL // tn

    def imap(*ids):
        j = 0 if n_axis is None else ids[n_axis]
        return (layer * MOD_ROWS + _group(ids[m_axis], tm), 0, chunk * nb + j)

    return pl.BlockSpec((None, 1, tn), imap)


def _rms(x, g):
    y = x * lax.rsqrt(jnp.mean(x * x, axis=-1, keepdims=True) + EPS)
    return y * g


def _dot(a, b):
    return jnp.dot(a, b, preferred_element_type=F32)


def _dot_nt(a, b):
    return lax.dot_general(a, b, (((1,), (1,)), ((), ())), preferred_element_type=F32)


def _rope(x, c, s):
    lane = lax.broadcasted_iota(jnp.int32, (x.shape[0], LANES), 1)
    first = (lane & 31) < 16
    outs = []
    for k in range(x.shape[1] // LANES):
        a = x[:, k * LANES:(k + 1) * LANES]
        partner = jnp.where(first, pltpu.roll(a, LANES - 16, 1), pltpu.roll(a, 16, 1))
        outs.append(a * c + partner * s)
    return outs[0] if len(outs) == 1 else jnp.concatenate(outs, axis=1)


def _ada_body(cond_ref, w_ref, b_ref, o_ref):
    s = jax.nn.silu(cond_ref[...])
    o_ref[...] = _dot(s.astype(BF16), w_ref[...].astype(BF16)) + b_ref[...]


def _ada_call(cond, ada_w, ada_b):
    tn = 1024
    n = 6 * D_MODEL
    return pl.pallas_call(
        _ada_body,
        grid=(DEPTH, n // tn),
        in_specs=[
            pl.BlockSpec((MOD_ROWS, D_MODEL), lambda l, j: (0, 0)),
            pl.BlockSpec((None, D_MODEL, tn), lambda l, j: (l, 0, j)),
            pl.BlockSpec((None, 1, tn), lambda l, j: (l, 0, j)),
        ],
        out_specs=pl.BlockSpec((None, MOD_ROWS, tn), lambda l, j: (l, 0, j)),
        out_shape=jax.ShapeDtypeStruct((DEPTH, MOD_ROWS, n), F32),
        compiler_params=_params(2),
        name="ada_modulation",
    )(cond, ada_w, ada_b.reshape(DEPTH, 1, n))


def _part_specs(n, tm, width):
    ctx_tiles = N_CTX // tm
    if n == 1:
        return [pl.BlockSpec((tm, width), lambda i: (i, 0))]
    return [pl.BlockSpec((tm, width), lambda i: (jnp.minimum(i, ctx_tiles - 1), 0)),
            pl.BlockSpec((tm, width), lambda i: (jnp.maximum(i - ctx_tiles, 0), 0))]


def _modulate_body(xc_ref, xl_ref, g_ref, sc_ref, sh_ref, o_ref, *, ctx_tiles):
    def run(x_ref):
        y = _rms(x_ref[...], g_ref[...])
        o_ref[...] = (y * (1.0 + sc_ref[...]) + sh_ref[...]).astype(o_ref.dtype)

    i = pl.program_id(0)
    pl.when(i < ctx_tiles)(lambda: run(xc_ref))
    pl.when(i >= ctx_tiles)(lambda: run(xl_ref))


def _modulate_call(x_ctx, x_lat, gains, layer, mod, shift_chunk, scale_chunk):
    tm = 512
    return pl.pallas_call(
        functools.partial(_modulate_body, ctx_tiles=N_CTX // tm),
        grid=(M_ALL // tm,),
        in_specs=_part_specs(2, tm, D_MODEL) + [
            pl.BlockSpec((None, 1, D_MODEL), lambda i: (layer, 0, 0)),
            _mod_spec(layer, scale_chunk, tm, D_MODEL, 0, None),
            _mod_spec(layer, shift_chunk, tm, D_MODEL, 0, None),
        ],
        out_specs=pl.BlockSpec((tm, D_MODEL), lambda i: (i, 0)),
        out_shape=jax.ShapeDtypeStruct((M_ALL, D_MODEL), BF16),
        compiler_params=_params(1),
        name="norm_modulate",
    )(x_ctx, x_lat, gains.reshape(DEPTH, 1, D_MODEL), mod, mod)


def _low_lanes(rows):
    return lax.broadcasted_iota(jnp.int32, (rows, LANES), 1) < HALF_LANES


def _cast_pair_order(w_ref, w_s):
    low = _low_lanes(w_ref.shape[0])
    per_pair = 2 * WIN_GROUP * WIN_HEAD_DIM // LANES
    for p in range(w_ref.shape[1] // (per_pair * LANES)):
        c = [w_ref[:, (per_pair * p + t) * LANES:(per_pair * p + t + 1) * LANES] for t in range(per_pair)]
        r = [pltpu.roll(x, HALF_LANES, 1) for x in c]
        moved = [jnp.where(low, c[0], r[2]), jnp.where(low, r[0], c[2]),
                 jnp.where(low, c[1], r[3]), jnp.where(low, r[1], c[3])]
        for t, x in enumerate(moved):
            w_s[:, (per_pair * p + t) * LANES:(per_pair * p + t + 1) * LANES] = x.astype(BF16)


PROJ_SPLIT = 4


def _proj_body(*refs, rope_lo, rope_hi, n_ctx_tiles, pair_order):
    has_rope = rope_hi > rope_lo
    if has_rope:
        lhs_ref, w_ref, cos_ref, sin_ref, o_ref, w_s = refs
    else:
        lhs_ref, w_ref, o_ref, w_s = refs
    j = pl.program_id(0)
    i = pl.program_id(1)

    @pl.when(i == 0)
    def _():
        if pair_order:
            _cast_pair_order(w_ref, w_s)
        else:
            w_s[...] = w_ref[...].astype(BF16)

    def run(with_rope):
        sub = lhs_ref.shape[0] // PROJ_SPLIT
        pending = None
        for s in range(PROJ_SPLIT + 1):
            if s < PROJ_SPLIT:
                rows = slice(s * sub, (s + 1) * sub)
                cur = (rows, _dot(lhs_ref[rows, :].astype(BF16), w_s[...]))
            if pending is not None:
                prow, acc = pending
                if with_rope:
                    acc = _rope(acc, cos_ref[prow, :], sin_ref[prow, :])
                o_ref[prow, :] = acc.astype(o_ref.dtype)
            pending = cur if s < PROJ_SPLIT else None

    if not has_rope:
        run(False)
    else:
        roped = jnp.logical_and(jnp.logical_and(j >= rope_lo, j < rope_hi), i >= n_ctx_tiles)
        pl.when(roped)(lambda: run(True))
        pl.when(jnp.logical_not(roped))(lambda: run(False))


def _proj_call(lhs, w, layer, col_off, n_cols, out_dtype, *, tm, tn, rope=None, tables=None,
               pair_order=False, name):
    m, k = lhs.shape
    n_tiles = n_cols // tn
    off = col_off // tn
    rope_lo, rope_hi = rope if rope is not None else (0, 0)
    if w.ndim == 3:
        w_spec = pl.BlockSpec((None, k, tn), lambda j, i: (layer, 0, off + j))
    else:
        w_spec = pl.BlockSpec((k, tn), lambda j, i: (0, off + j))
    in_specs = [pl.BlockSpec((tm, k), lambda j, i: (i, 0)), w_spec]
    args = [lhs, w]
    if rope_hi > rope_lo:
        in_specs += [pl.BlockSpec((tm, LANES), lambda j, i: (i, 0))] * 2
        args += list(tables)
    return pl.pallas_call(
        functools.partial(_proj_body, rope_lo=rope_lo, rope_hi=rope_hi, n_ctx_tiles=N_CTX // tm,
                          pair_order=pair_order),
        grid=(n_tiles, m // tm),
        in_specs=in_specs,
        out_specs=pl.BlockSpec((tm, tn), lambda j, i: (i, j)),
        out_shape=jax.ShapeDtypeStruct((m, n_cols), out_dtype),
        scratch_shapes=[pltpu.VMEM((k, tn), BF16)],
        compiler_params=_params(2),
        name=name,
    )(*args)


ROW_BLOCK = WIN_HEAD_DIM


def _resid_norm_body(*refs, n_parts, n_x, ctx_tiles, cast_w, row_perm, final):
    lhs_refs = refs[:n_parts]
    w_ref = refs[n_parts]
    x_refs = refs[n_parts + 1:n_parts + 1 + n_x]
    n_mod = 2 if final else 4
    mods = refs[n_parts + 1 + n_x:n_parts + 1 + n_x + n_mod]
    outs = refs[n_parts + 1 + n_x + n_mod:n_parts + 1 + n_x + n_mod + 2]
    scratch = refs[n_parts + 1 + n_x + n_mod + 2:]
    gate_ref, g_ref = mods[:2]
    i = pl.program_id(0)
    if cast_w:
        w = scratch[0]

        @pl.when(i == 0)
        def _():
            if row_perm is None:
                w[...] = w_ref[...].astype(BF16)
            else:
                for new, old in enumerate(row_perm):
                    w[new * ROW_BLOCK:(new + 1) * ROW_BLOCK, :] = (
                        w_ref[old * ROW_BLOCK:(old + 1) * ROW_BLOCK, :].astype(BF16))
    else:
        w = w_ref

    def step(lhs_ref, x_ref, y_ref):
        xn = x_ref[...] + gate_ref[...] * _dot(lhs_ref[...], w[...])
        y = _rms(xn, g_ref[...])
        if final:
            y_ref[...] = y
        else:
            outs[0][...] = xn
            outs[1][...] = (y * (1.0 + mods[2][...]) + mods[3][...]).astype(outs[1].dtype)

    if n_parts == 1 and n_x == 1 and not final:
        step(lhs_refs[0], x_refs[0], None)
    else:
        pl.when(i < ctx_tiles)(lambda: step(lhs_refs[0], x_refs[0], outs[0]))
        pl.when(i >= ctx_tiles)(lambda: step(lhs_refs[-1], x_refs[-1], outs[1]))


def _resid_norm_call(lhs_parts, w, w_layer, x_parts, mod, layer, gate_chunk, gains, gain_layer, next_mod, *,
                     tm, row_perm=None, name):
    k = lhs_parts[0].shape[1]
    n_parts = len(lhs_parts)
    n_x = len(x_parts)
    ctx_tiles = N_CTX // tm
    final = next_mod is None
    cast_w = w.dtype != BF16
    if w.ndim == 3:
        w_spec = pl.BlockSpec((None, k, D_MODEL), lambda i: (w_layer, 0, 0), pipeline_mode=pl.Buffered(1))
    else:
        w_spec = pl.BlockSpec((k, D_MODEL), lambda i: (0, 0))
    in_specs = _part_specs(n_parts, tm, k) + [w_spec] + _part_specs(n_x, tm, D_MODEL) + [
        _mod_spec(layer, gate_chunk, tm, D_MODEL, 0, None),
        pl.BlockSpec((None, 1, D_MODEL), lambda i: (gain_layer, 0, 0)),
    ]
    args = list(lhs_parts) + [w] + list(x_parts) + [mod, gains]
    if final:
        out_specs = _part_specs(2, tm, D_MODEL)
        out_shape = [jax.ShapeDtypeStruct((N_CTX, D_MODEL), F32), jax.ShapeDtypeStruct((N_LAT, D_MODEL), F32)]
    else:
        nl, shift_chunk, scale_chunk = next_mod
        in_specs += [_mod_spec(nl, scale_chunk, tm, D_MODEL, 0, None),
                     _mod_spec(nl, shift_chunk, tm, D_MODEL, 0, None)]
        args += [mod, mod]
        out_specs = _part_specs(1, tm, D_MODEL) * 2
        out_shape = [jax.ShapeDtypeStruct((M_ALL, D_MODEL), F32), jax.ShapeDtypeStruct((M_ALL, D_MODEL), BF16)]
    return pl.pallas_call(
        functools.partial(_resid_norm_body, n_parts=n_parts, n_x=n_x, ctx_tiles=ctx_tiles, cast_w=cast_w,
                          row_perm=row_perm, final=final),
        grid=(M_ALL // tm,),
        in_specs=in_specs,
        out_specs=out_specs,
        out_shape=out_shape,
        scratch_shapes=[pltpu.VMEM((k, D_MODEL), BF16)] if cast_w else [],
        compiler_params=_params(1),
        name=name,
    )(*args)


SWIGLU_SPLIT = 8


def _swiglu_body(h_ref, wg_ref, wu_ref, wd_ref, o_ref, wdo_ref, wg_s, wu_s):
    @pl.when(pl.program_id(1) == 0)
    def _():
        wg_s[...] = wg_ref[...].astype(BF16)
        wu_s[...] = wu_ref[...].astype(BF16)

    sub = h_ref.shape[0] // SWIGLU_SPLIT
    pending = None
    for s in range(SWIGLU_SPLIT + 1):
        if s < SWIGLU_SPLIT:
            rows = slice(s * sub, (s + 1) * sub)
            h = h_ref[rows, :]
            cur = (rows, _dot(h, wg_s[...]), _dot(h, wu_s[...]))
        if pending is not None:
            prow, g, u = pending
            o_ref[prow, :] = (jax.nn.silu(g) * u).astype(o_ref.dtype)
        pending = cur if s < SWIGLU_SPLIT else None
    wdo_ref[...] = wd_ref[...].astype(BF16)


def _swiglu_call(h, w_gate, w_up, w_down, layer):
    tm, tn = 2048, 512
    n_i = M_ALL // tm
    slab = D_FF // ((D_FF // tn) * n_i)
    return pl.pallas_call(
        _swiglu_body,
        grid=(D_FF // tn, n_i),
        in_specs=[
            pl.BlockSpec((tm, D_MODEL), lambda j, i: (i, 0)),
            pl.BlockSpec((None, D_MODEL, tn), lambda j, i: (layer, 0, j)),
            pl.BlockSpec((None, D_MODEL, tn), lambda j, i: (layer, 0, j)),
            pl.BlockSpec((None, slab, D_MODEL), lambda j, i: (layer, j * n_i + i, 0)),
        ],
        out_specs=[
            pl.BlockSpec((tm, tn), lambda j, i: (i, j)),
            pl.BlockSpec((slab, D_MODEL), lambda j, i: (j * n_i + i, 0)),
        ],
        out_shape=[jax.ShapeDtypeStruct((M_ALL, D_FF), BF16), jax.ShapeDtypeStruct((D_FF, D_MODEL), BF16)],
        scratch_shapes=[pltpu.VMEM((D_MODEL, tn), BF16), pltpu.VMEM((D_MODEL, tn), BF16)],
        compiler_params=_params(2, vmem=VMEM_LIMIT_LARGE),
        name="ffn_gate_up",
    )(h, w_gate, w_up, w_down)


def _mla_in_body(h_ref, w_ref, qn_ref, kvn_ref, cos_ref, sin_ref, cq_ref, ckv_ref, kpe_ref, w_s):
    @pl.when(pl.program_id(0) == 0)
    def _():
        w_s[...] = w_ref[...].astype(BF16)

    def epilogue(rows, acc):
        cq = acc[:, :Q_LORA_RANK]
        ckv = acc[:, Q_LORA_RANK:Q_LORA_RANK + KV_LORA_RANK]
        kpe = acc[:, Q_LORA_RANK + KV_LORA_RANK:]
        kpe = jnp.concatenate([kpe, kpe], axis=1)
        cq_ref[rows, :] = _rms(cq, qn_ref[...]).astype(cq_ref.dtype)
        ckv_ref[rows, :] = _rms(ckv, kvn_ref[...])
        kpe_ref[rows, :] = _rope(kpe, cos_ref[rows, :], sin_ref[rows, :])

    sub = h_ref.shape[0] // PROJ_SPLIT
    pending = None
    for s in range(PROJ_SPLIT + 1):
        if s < PROJ_SPLIT:
            rows = slice(s * sub, (s + 1) * sub)
            cur = (rows, _dot(h_ref[rows, :], w_s[...]))
        if pending is not None:
            epilogue(*pending)
        pending = cur if s < PROJ_SPLIT else None


def _mla_in_call(h, w_in, layer, q_norm, kv_norm, tables):
    tm = 1024
    row = lambda i: (i, 0)
    fixed = lambda i: (0, 0)
    n_in = Q_LORA_RANK + KV_LORA_RANK + QK_ROPE_DIM
    return pl.pallas_call(
        _mla_in_body,
        grid=(M_ALL // tm,),
        in_specs=[
            pl.BlockSpec((tm, D_MODEL), row),
            pl.BlockSpec((None, D_MODEL, n_in), lambda i: (layer, 0, 0), pipeline_mode=pl.Buffered(1)),
            pl.BlockSpec((1, Q_LORA_RANK), fixed),
            pl.BlockSpec((1, KV_LORA_RANK), fixed),
            pl.BlockSpec((tm, LANES), row),
            pl.BlockSpec((tm, LANES), row),
        ],
        out_specs=[
            pl.BlockSpec((tm, Q_LORA_RANK), row),
            pl.BlockSpec((tm, KV_LORA_RANK), row),
            pl.BlockSpec((tm, LANES), row),
        ],
        out_shape=[
            jax.ShapeDtypeStruct((M_ALL, Q_LORA_RANK), BF16),
            jax.ShapeDtypeStruct((M_ALL, KV_LORA_RANK), F32),
            jax.ShapeDtypeStruct((M_ALL, LANES), F32),
        ],
        scratch_shapes=[pltpu.VMEM((D_MODEL, n_in), BF16)],
        compiler_params=_params(1),
        name="mla_in_proj",
    )(h, w_in, q_norm.reshape(1, -1), kv_norm.reshape(1, -1), *tables)


def _win_heads(q_ref, o_ref, sink_ref, layer, k_chunks, v_chunks, blocks, nq, pipelined):
    low_q = _low_lanes(nq)
    qlo = jnp.where(low_q, WIN_SCALE, 0.0).astype(BF16)
    qhi = jnp.where(low_q, 0.0, WIN_SCALE).astype(BF16)
    heads_per_pair = 2 * WIN_GROUP

    def scores(p):
        kc = k_chunks[p].astype(BF16)
        qcs = [q_ref[:, (WIN_GROUP * p + t) * LANES:(WIN_GROUP * p + t + 1) * LANES] for t in range(WIN_GROUP)]
        qs = jnp.concatenate([qc * qlo for qc in qcs] + [qc * qhi for qc in qcs], axis=0)
        return _dot_nt(qs, kc)

    def softmax(p, lg):
        probs, invs = [], []
        for u in range(heads_per_pair):
            sk = sink_ref[layer, heads_per_pair * p + u]
            l = lg[u * nq:(u + 1) * nq]
            cols = [l[:, a:a + LANES] if msk is None else jnp.where(msk, l[:, a:a + LANES], NEG)
                    for a, msk in blocks]
            mx = cols[0]
            for col in cols[1:]:
                mx = jnp.maximum(mx, col)
            m = jnp.maximum(jnp.max(mx, axis=-1, keepdims=True), sk)
            es = [jnp.exp(col - m) for col in cols]
            tot = es[0]
            for e in es[1:]:
                tot = tot + e
            invs.append(1.0 / (jnp.sum(tot, axis=-1, keepdims=True) + jnp.exp(sk - m)))
            probs.append(jnp.concatenate([e.astype(BF16) for e in es], axis=1))
        return jnp.concatenate(probs, axis=0), invs

    def values(p, probs, invs):
        o = _dot(probs, v_chunks[p].astype(BF16))
        on = [o[u * nq:(u + 1) * nq] * invs[u] for u in range(heads_per_pair)]
        for t in range(WIN_GROUP):
            c = WIN_GROUP * p + t
            o_ref[:, c * LANES:(c + 1) * LANES] = jnp.where(low_q, on[t], on[WIN_GROUP + t]).astype(o_ref.dtype)

    n = WIN_KV_HEADS // 2
    if not pipelined:
        for p in range(n):
            values(p, *softmax(p, scores(p)))
        return
    sc, pr = {}, {}
    for p in range(n + 2):
        if p < n:
            sc[p] = scores(p)
        if 0 <= p - 1 < n:
            pr[p - 1] = softmax(p - 1, sc.pop(p - 1))
        if 0 <= p - 2 < n:
            values(p - 2, *pr.pop(p - 2))


def _win_ctx_body(sink_ref, q_ref, kv_ref, o_ref, *, layer):
    n_pairs = WIN_KD // LANES
    k_chunks = [kv_ref[:, p * LANES:(p + 1) * LANES] for p in range(n_pairs)]
    v_chunks = [kv_ref[:, WIN_KD + p * LANES:WIN_KD + (p + 1) * LANES] for p in range(n_pairs)]
    _win_heads(q_ref, o_ref, sink_ref, layer, k_chunks, v_chunks,
               [(a, None) for a in range(0, SEQ, LANES)], SEQ, False)


def _win_ctx_call(q, kv, sink, layer):
    return pl.pallas_call(
        functools.partial(_win_ctx_body, layer=layer),
        grid=(BATCH,),
        in_specs=[
            pl.BlockSpec(memory_space=pltpu.SMEM),
            pl.BlockSpec((SEQ, WIN_QD), lambda b: (b, 0)),
            pl.BlockSpec((SEQ, 2 * WIN_KD), lambda b: (b, 0)),
        ],
        out_specs=pl.BlockSpec((SEQ, WIN_QD), lambda b: (b, 0)),
        out_shape=jax.ShapeDtypeStruct((N_CTX, WIN_QD), BF16),
        compiler_params=_params(1),
        name="win_attn_context",
    )(sink, q, kv)


def _win_lat_body(sink_ref, q_ref, kvp_ref, kvc_ref, kvn_ref, ck_ref, cv_ref, o_ref, *, layer):
    n = pl.program_id(1)
    n_pairs = WIN_KD // LANES
    k_chunks, v_chunks = [], []
    for p in range(n_pairs):
        ks = slice(p * LANES, (p + 1) * LANES)
        vs = slice(WIN_KD + p * LANES, WIN_KD + (p + 1) * LANES)
        k_chunks.append(jnp.concatenate([kvp_ref[:, ks], kvc_ref[:, ks], kvn_ref[:, ks], ck_ref[:, ks]], axis=0))
        v_chunks.append(jnp.concatenate([kvp_ref[:, vs], kvc_ref[:, vs], kvn_ref[:, vs], cv_ref[:, ks]], axis=0))
    r = lax.broadcasted_iota(jnp.int32, (BLOCK, BLOCK), 0)
    s = lax.broadcasted_iota(jnp.int32, (BLOCK, BLOCK), 1)
    prev_ok = (s - r) >= jnp.where(n >= 1, 0, BLOCK)
    next_ok = (r - s) >= jnp.where(n <= DEC_SEQ // BLOCK - 2, 0, BLOCK)
    blocks = [(0, prev_ok), (BLOCK, None), (2 * BLOCK, next_ok)]
    blocks += [(3 * BLOCK + a, None) for a in range(0, PAST_LEN, LANES)]
    _win_heads(q_ref, o_ref, sink_ref, layer, k_chunks, v_chunks, blocks, BLOCK, True)


def _win_lat_call(q, kv, cache_k, cache_v, sink, layer):
    nb = DEC_SEQ // BLOCK
    base = N_CTX // BLOCK

    def rows(shift):
        return lambda b, n: (base + b * nb + jnp.clip(n + shift, 0, nb - 1), 0)

    cache_spec = pl.BlockSpec((None, None, PAST_LEN, WIN_KD), lambda b, n: (b, layer, 0, 0))
    return pl.pallas_call(
        functools.partial(_win_lat_body, layer=layer),
        grid=(DEC_BATCH, nb),
        in_specs=[
            pl.BlockSpec(memory_space=pltpu.SMEM),
            pl.BlockSpec((BLOCK, WIN_QD), rows(0)),
            pl.BlockSpec((BLOCK, 2 * WIN_KD), rows(-1)),
            pl.BlockSpec((BLOCK, 2 * WIN_KD), rows(0)),
            pl.BlockSpec((BLOCK, 2 * WIN_KD), rows(1)),
            cache_spec,
            cache_spec,
        ],
        out_specs=pl.BlockSpec((BLOCK, WIN_QD), lambda b, n: (b * nb + n, 0)),
        out_shape=jax.ShapeDtypeStruct((N_LAT, WIN_QD), BF16),
        compiler_params=_params(2),
        name="win_attn_latent",
    )(sink, q, kv, kv, kv, cache_k, cache_v)


KV_HEAD_W = QK_NOPE_DIM + V_HEAD_DIM
MLA_NOPE_W = MLA_HEADS * QK_NOPE_DIM
MLA_ROPE_W = MLA_HEADS * QK_ROPE_DIM
MLA_EXP2_SCALE = MLA_SCALE * LOG2E


def _half_masks_bf16(rows):
    low = _low_lanes(rows)
    return jnp.where(low, 1.0, 0.0).astype(BF16), jnp.where(low, 0.0, 1.0).astype(BF16)


def _mla_ctx_body(qn_ref, qp_ref, kvx_ref, kpe_ref, o_ref):
    qlo, qhi = _half_masks_bf16(SEQ)
    kpe2 = kpe_ref[...].astype(BF16)
    for h in range(MLA_HEADS):
        qp = qp_ref[:, (h // 2) * LANES:(h // 2 + 1) * LANES] * (qlo if h % 2 == 0 else qhi)
        qcat = jnp.concatenate([qn_ref[:, h * LANES:(h + 1) * LANES], qp], axis=1)
        kcat = jnp.concatenate([kvx_ref[:, h * KV_HEAD_W:h * KV_HEAD_W + QK_NOPE_DIM], kpe2], axis=1)
        lg = _dot_nt(qcat, kcat)
        m = jnp.max(lg, axis=-1, keepdims=True)
        e = jnp.exp2((lg - m) * MLA_EXP2_SCALE)
        inv = 1.0 / jnp.sum(e, axis=-1, keepdims=True)
        o = _dot(e.astype(BF16), kvx_ref[:, h * KV_HEAD_W + QK_NOPE_DIM:(h + 1) * KV_HEAD_W])
        o_ref[:, h * LANES:(h + 1) * LANES] = (o * inv).astype(o_ref.dtype)


def _mla_ctx_call(q, kvx, kpe):
    return pl.pallas_call(
        _mla_ctx_body,
        grid=(BATCH,),
        in_specs=[
            pl.BlockSpec((SEQ, MLA_NOPE_W), lambda b: (b, 0)),
            pl.BlockSpec((SEQ, MLA_ROPE_W), lambda b: (b, MLA_NOPE_W // MLA_ROPE_W)),
            pl.BlockSpec((SEQ, MLA_HEADS * KV_HEAD_W), lambda b: (b, 0)),
            pl.BlockSpec((SEQ, LANES), lambda b: (b, 0)),
        ],
        out_specs=pl.BlockSpec((SEQ, MLA_HEADS * V_HEAD_DIM), lambda b: (b, 0)),
        out_shape=jax.ShapeDtypeStruct((N_CTX, MLA_HEADS * V_HEAD_DIM), BF16),
        compiler_params=_params(1),
        name="mla_attn_context",
    )(q, q, kvx, kpe)


MLA_QB = 2048
MLA_SUB = 512


def _mla_lat_body(qn_ref, qp_ref, kvl_ref, kvc_ref, kpl_ref, kpc_ref, o_ref):
    qlo, qhi = _half_masks_bf16(MLA_SUB)
    kpl = kpl_ref[...].astype(BF16)
    kpc = kpc_ref[...].astype(BF16)
    qp_pair = qp_ref[...]
    chains = [(t, r) for t in range(2) for r in range(MLA_QB // MLA_SUB)]

    def scores(t, r):
        c0 = t * KV_HEAD_W
        rows = slice(r * MLA_SUB, (r + 1) * MLA_SUB)
        qcat = jnp.concatenate([qn_ref[rows, t * LANES:(t + 1) * LANES],
                                qp_pair[rows] * (qlo if t == 0 else qhi)], axis=1)
        k_lat = jnp.concatenate([kvl_ref[:, c0:c0 + QK_NOPE_DIM], kpl], axis=1)
        k_cache = jnp.concatenate([kvc_ref[:, c0:c0 + QK_NOPE_DIM], kpc], axis=1)
        return _dot_nt(qcat, k_lat), _dot_nt(qcat, k_cache)

    def softmax(l1, l2):
        m = jnp.maximum(jnp.max(l1, axis=-1, keepdims=True), jnp.max(l2, axis=-1, keepdims=True))
        e1 = jnp.exp2((l1 - m) * MLA_EXP2_SCALE)
        e2 = jnp.exp2((l2 - m) * MLA_EXP2_SCALE)
        inv = 1.0 / (jnp.sum(e1, axis=-1, keepdims=True) + jnp.sum(e2, axis=-1, keepdims=True))
        return e1.astype(BF16), e2.astype(BF16), inv

    def values(t, r, p1, p2, inv):
        c0 = t * KV_HEAD_W
        o = (_dot(p1, kvl_ref[:, c0 + QK_NOPE_DIM:c0 + KV_HEAD_W])
             + _dot(p2, kvc_ref[:, c0 + QK_NOPE_DIM:c0 + KV_HEAD_W]))
        o_ref[r * MLA_SUB:(r + 1) * MLA_SUB, t * LANES:(t + 1) * LANES] = (o * inv).astype(o_ref.dtype)

    n = len(chains)
    sc, pr = {}, {}
    for c in range(n + 2):
        if c < n:
            sc[c] = scores(*chains[c])
        if 0 <= c - 1 < n:
            pr[c - 1] = softmax(*sc.pop(c - 1))
        if 0 <= c - 2 < n:
            values(*chains[c - 2], *pr.pop(c - 2))


def _mla_lat_call(q, kvx, kvx_cache, kpe, kpe_cache):
    n_qb = DEC_SEQ // MLA_QB
    n_pairs = MLA_HEADS // 2
    lat_blk = N_CTX // DEC_SEQ
    return pl.pallas_call(
        _mla_lat_body,
        grid=(DEC_BATCH, n_pairs, n_qb),
        in_specs=[
            pl.BlockSpec((MLA_QB, 2 * QK_NOPE_DIM), lambda b, hp, qb: (N_CTX // MLA_QB + b * n_qb + qb, hp)),
            pl.BlockSpec((MLA_QB, LANES), lambda b, hp, qb: (N_CTX // MLA_QB + b * n_qb + qb, MLA_NOPE_W // LANES + hp)),
            pl.BlockSpec((DEC_SEQ, 2 * KV_HEAD_W), lambda b, hp, qb: (lat_blk + b, hp)),
            pl.BlockSpec((PAST_LEN, 2 * KV_HEAD_W), lambda b, hp, qb: (b, hp)),
            pl.BlockSpec((DEC_SEQ, LANES), lambda b, hp, qb: (lat_blk + b, 0)),
            pl.BlockSpec((None, PAST_LEN, LANES), lambda b, hp, qb: (b, 0, 0)),
        ],
        out_specs=pl.BlockSpec((MLA_QB, 2 * V_HEAD_DIM), lambda b, hp, qb: (b * n_qb + qb, hp)),
        out_shape=jax.ShapeDtypeStruct((N_LAT, MLA_HEADS * V_HEAD_DIM), BF16),
        compiler_params=_params(3),
        name="mla_attn_latent",
    )(q, q, kvx, kvx_cache, kpe, kpe_cache)


def _pack_body(*refs, n_layers, col_ranges):
    ins, outs = refs[:-len(col_ranges)], refs[-len(col_ranges):]
    for o_ref, (src, lo, hi) in zip(outs, col_ranges):
        for l in range(n_layers):
            o_ref[0, l] = ins[src * n_layers + l][:, lo:hi]


def _pack_call(sources, col_ranges, name):
    n_layers = len(sources[0])
    flat = [a for src in sources for a in src]
    return pl.pallas_call(
        functools.partial(_pack_body, n_layers=n_layers, col_ranges=col_ranges),
        grid=(BATCH,),
        in_specs=[pl.BlockSpec((SEQ, a.shape[1]), lambda b: (b, 0)) for a in flat],
        out_specs=[pl.BlockSpec((1, n_layers, SEQ, hi - lo), lambda b: (b, 0, 0, 0)) for _, lo, hi in col_ranges],
        out_shape=[jax.ShapeDtypeStruct((BATCH, n_layers, SEQ, hi - lo), F32) for _, lo, hi in col_ranges],
        compiler_params=_params(1),
        name=name,
    )(*flat)


def _rope_tables():
    t = jnp.arange(DEC_SEQ, dtype=jnp.int32)
    rows, cols = t // GRID_W, t % GRID_W
    half = WIN_HEAD_DIM // 4
    freqs = ROPE_BASE ** (-jnp.arange(half, dtype=F32) / half)
    ang_r = rows.astype(F32)[:, None] * freqs[None, :]
    ang_c = cols.astype(F32)[:, None] * freqs[None, :]
    cr, sr, cc, sc = jnp.cos(ang_r), jnp.sin(ang_r), jnp.cos(ang_c), jnp.sin(ang_c)
    c64 = jnp.concatenate([cr, cr, cc, cc], axis=-1)
    s64 = jnp.concatenate([-sr, sr, -sc, sc], axis=-1)
    c_lat = jnp.tile(c64, (DEC_BATCH, LANES // WIN_HEAD_DIM))
    s_lat = jnp.tile(s64, (DEC_BATCH, LANES // WIN_HEAD_DIM))
    cos_t = jnp.concatenate([jnp.ones((N_CTX, LANES), F32), c_lat], axis=0)
    sin_t = jnp.concatenate([jnp.zeros((N_CTX, LANES), F32), s_lat], axis=0)
    return cos_t, sin_t


def kernel(x_prompt, x_sample, cache_win_k, cache_win_v, cache_mla_ckv, cache_mla_kpe, c, c_ctx, ada_w, ada_b, norm_mix, norm_ffn, win_w_qkv, win_w_o, win_sink, mla_w_in, mla_q_norm, mla_w_q_b, mla_kv_norm, mla_w_kv_b, mla_w_o, ffn_w_gate, ffn_w_up, ffn_w_down, norm_final):
    x = [x_prompt.reshape(N_CTX, D_MODEL), x_sample.reshape(N_LAT, D_MODEL)]
    cond = jnp.concatenate([c_ctx[None, :], c, jnp.zeros((MOD_ROWS - 1 - DEC_BATCH, D_MODEL), F32)], axis=0)
    mod = _ada_call(cond, ada_w, ada_b).reshape(DEPTH * MOD_ROWS, 1, 6 * D_MODEL)
    tables = _rope_tables()
    cache_k = cache_win_k.reshape(DEC_BATCH, N_WIN_LAYERS, PAST_LEN, WIN_KD)
    cache_v = cache_win_v.reshape(DEC_BATCH, N_WIN_LAYERS, PAST_LEN, WIN_KD)

    kv_layers, ckv_layers, kpe_layers = [], [], []
    mix_gains = norm_mix.reshape(DEPTH, 1, D_MODEL)
    ffn_gains = norm_ffn.reshape(DEPTH, 1, D_MODEL)
    win_row_perm = [8 * p + 4 * s + t for p in range(WIN_KV_HEADS // 2) for t in range(WIN_GROUP) for s in range(2)]
    h = _modulate_call(x[0], x[1], norm_mix, 0, mod, 0, 1)
    for layer in range(DEPTH):
        j = layer // 2
        if layer % 2 == 0:
            q = _proj_call(h, win_w_qkv, j, 0, WIN_QD, BF16, tm=1024, tn=1024,
                           rope=(0, WIN_QD // 1024), tables=tables, pair_order=True, name="win_q_proj")
            kv = _proj_call(h, win_w_qkv, j, WIN_QD, 2 * WIN_KD, F32, tm=1024, tn=512,
                            rope=(0, 1), tables=tables, name="win_kv_proj")
            o_ctx = _win_ctx_call(q, kv, win_sink, j)
            o_lat = _win_lat_call(q, kv, cache_k, cache_v, win_sink, j)
            w_o, row_perm = win_w_o, win_row_perm
            kv_layers.append(kv)
        else:
            cq, ckv, kpe = _mla_in_call(h, mla_w_in, j, mla_q_norm[j], mla_kv_norm[j], tables)
            wq = mla_w_q_b[j].reshape(Q_LORA_RANK, MLA_HEADS, QK_NOPE_DIM + QK_ROPE_DIM)
            wq = jnp.concatenate([wq[:, :, :QK_NOPE_DIM].reshape(Q_LORA_RANK, MLA_NOPE_W),
                                  wq[:, :, QK_NOPE_DIM:].reshape(Q_LORA_RANK, MLA_ROPE_W)], axis=1)
            q = _proj_call(cq, wq, 0, 0, MLA_NOPE_W + MLA_ROPE_W, BF16, tm=1024, tn=1024,
                           rope=(MLA_NOPE_W // 1024, (MLA_NOPE_W + MLA_ROPE_W) // 1024), tables=tables,
                           name="mla_q_proj")
            kvx = _proj_call(ckv, mla_w_kv_b, j, 0, MLA_HEADS * KV_HEAD_W, BF16, tm=1024, tn=1024,
                             name="mla_kv_expand")
            cache_ckv = cache_mla_ckv[:, j].reshape(DEC_BATCH * PAST_LEN, KV_LORA_RANK)
            kvx_cache = _proj_call(cache_ckv, mla_w_kv_b, j, 0, MLA_HEADS * KV_HEAD_W, BF16, tm=512, tn=1024,
                                   name="mla_kv_expand_cache")
            cache_kpe = jnp.tile(cache_mla_kpe[:, j], (1, 1, LANES // QK_ROPE_DIM))
            o_ctx = _mla_ctx_call(q, kvx, kpe)
            o_lat = _mla_lat_call(q, kvx, kvx_cache, kpe, cache_kpe)
            w_o, row_perm = mla_w_o, None
            ckv_layers.append(ckv)
            kpe_layers.append(kpe)
        xn, h = _resid_norm_call([o_ctx, o_lat], w_o, j, x, mod, layer, 2, ffn_gains, layer, (layer, 3, 4),
                                 tm=256, row_perm=row_perm, name="mixer_out_proj")
        act, w_down = _swiglu_call(h, ffn_w_gate, ffn_w_up, ffn_w_down, layer)
        if layer + 1 < DEPTH:
            xn, h = _resid_norm_call([act], w_down, 0, [xn], mod, layer, 5, mix_gains, layer + 1,
                                     (layer + 1, 0, 1), tm=256, name="ffn_down_proj")
            x = [xn]
        else:
            y_ctx, y_lat = _resid_norm_call([act], w_down, 0, [xn], mod, layer, 5,
                                            norm_final.reshape(1, 1, D_MODEL), 0, None,
                                            tm=256, name="ffn_down_final")

    y_prompt = y_ctx.reshape(BATCH, SEQ, D_MODEL)
    y_sample = y_lat.reshape(DEC_BATCH, DEC_SEQ, D_MODEL)
    new_k, new_v = _pack_call([kv_layers], [(0, 0, WIN_KD), (0, WIN_KD, 2 * WIN_KD)], "pack_win_cache")
    new_ckv, new_kpe = _pack_call([ckv_layers, kpe_layers], [(0, 0, KV_LORA_RANK), (1, 0, QK_ROPE_DIM)],
                                  "pack_mla_cache")
    head_shape = (BATCH, N_WIN_LAYERS, SEQ, WIN_KV_HEADS, WIN_HEAD_DIM)
    return (y_prompt, y_sample, new_k.reshape(head_shape), new_v.reshape(head_shape), new_ckv, new_kpe)
```

```python
import functools
import math

import numpy as np
import jax
import jax.numpy as jnp
from jax import lax
from jax.experimental import pallas as pl
from jax.experimental.pallas import tpu as pltpu

F32 = jnp.float32
BF16 = jnp.bfloat16

D_MODEL = 2048
BATCH = 16
SEQ = 256
DEPTH = 4
DEC_BATCH = 2
DEC_SEQ = 2048
PAST_LEN = 256
GRID_W = 64
N_WIN_LAYERS = 2
N_MLA_LAYERS = 2
WIN_HEADS = 32
WIN_KV_HEADS = 8
WIN_GROUP = WIN_HEADS // WIN_KV_HEADS
WIN_HEAD_DIM = 64
WINDOW = 128
BLOCK = 128
WIN_SCALE = WIN_HEAD_DIM ** -0.5
MLA_HEADS = 16
Q_LORA_RANK = 512
KV_LORA_RANK = 512
QK_NOPE_DIM = 128
QK_ROPE_DIM = 64
V_HEAD_DIM = 128
MLA_SCALE = (QK_NOPE_DIM + QK_ROPE_DIM) ** -0.5
D_FF = 5632
ROPE_BASE = 10000.0
EPS = 1e-6
NEG = float(np.finfo(np.float32).min)
LOG2E = math.log2(math.e)

N_CTX = BATCH * SEQ
N_LAT = DEC_BATCH * DEC_SEQ
M_ALL = N_CTX + N_LAT
MOD_ROWS = 16
LANES = 128
HALF_LANES = LANES // 2
WIN_QD = WIN_HEADS * WIN_HEAD_DIM
WIN_KD = WIN_KV_HEADS * WIN_HEAD_DIM
VMEM_LIMIT = 56 * 1024 * 1024
VMEM_LIMIT_LARGE = 60 * 1024 * 1024


def _params(n_axes, vmem=VMEM_LIMIT):
    return pltpu.CompilerParams(dimension_semantics=("arbitrary",) * n_axes,
                                vmem_limit_bytes=vmem)


def _group(i, tm):
    start = i * tm
    return jnp.where(start < N_CTX, 0, 1 + (start - N_CTX) // DEC_SEQ)


def _mod_spec(chunk, tm, tn, m_axis, n_axis):
    nb = D_MODEL // tn

    def imap(*ids):
        j = 0 if n_axis is None else ids[n_axis]
        return (_group(ids[m_axis], tm), 0, chunk * nb + j)

    return pl.BlockSpec((None, 1, tn), imap)


def _rms(x, g):
    y = x * lax.rsqrt(jnp.mean(x * x, axis=-1, keepdims=True) + EPS)
    return y * g


def _dot(a, b):
    return jnp.dot(a, b, preferred_element_type=F32)


def _dot_nt(a, b):
    return lax.dot_general(a, b, (((1,), (1,)), ((), ())), preferred_element_type=F32)


def _rope(x, c, s):
    lane = lax.broadcasted_iota(jnp.int32, (x.shape[0], LANES), 1)
    first = (lane & 31) < 16
    outs = []
    for k in range(x.shape[1] // LANES):
        a = x[:, k * LANES:(k + 1) * LANES]
        partner = jnp.where(first, pltpu.roll(a, LANES - 16, 1), pltpu.roll(a, 16, 1))
        outs.append(a * c + partner * s)
    return outs[0] if len(outs) == 1 else jnp.concatenate(outs, axis=1)


def _ada_body(cond_ref, w_ref, b_ref, o_ref):
    s = jax.nn.silu(cond_ref[...])
    o_ref[...] = _dot(s.astype(BF16), w_ref[...].astype(BF16)) + b_ref[...]


def _ada_call(cond, ada_w, ada_b, layer):
    tn = 1024
    n = 6 * D_MODEL
    return pl.pallas_call(
        _ada_body,
        grid=(n // tn,),
        in_specs=[
            pl.BlockSpec((MOD_ROWS, D_MODEL), lambda j: (0, 0)),
            pl.BlockSpec((None, D_MODEL, tn), lambda j: (layer, 0, j)),
            pl.BlockSpec((None, 1, tn), lambda j: (layer, 0, j)),
        ],
        out_specs=pl.BlockSpec((MOD_ROWS, tn), lambda j: (0, j)),
        out_shape=jax.ShapeDtypeStruct((MOD_ROWS, n), F32),
        compiler_params=_params(1),
        name="ada_modulation",
    )(cond, ada_w, ada_b.reshape(DEPTH, 1, n))


def _part_specs(n, tm, width):
    ctx_tiles = N_CTX // tm
    if n == 1:
        return [pl.BlockSpec((tm, width), lambda i: (i, 0))]
    return [pl.BlockSpec((tm, width), lambda i: (jnp.minimum(i, ctx_tiles - 1), 0)),
            pl.BlockSpec((tm, width), lambda i: (jnp.maximum(i - ctx_tiles, 0), 0))]


def _modulate_body(xc_ref, xl_ref, g_ref, sc_ref, sh_ref, o_ref, *, ctx_tiles):
    def run(x_ref):
        y = _rms(x_ref[...], g_ref[...])
        o_ref[...] = (y * (1.0 + sc_ref[...]) + sh_ref[...]).astype(o_ref.dtype)

    i = pl.program_id(0)
    pl.when(i < ctx_tiles)(lambda: run(xc_ref))
    pl.when(i >= ctx_tiles)(lambda: run(xl_ref))


def _modulate_call(x_ctx, x_lat, gains, layer, mod, shift_chunk, scale_chunk):
    tm = 512
    return pl.pallas_call(
        functools.partial(_modulate_body, ctx_tiles=N_CTX // tm),
        grid=(M_ALL // tm,),
        in_specs=_part_specs(2, tm, D_MODEL) + [
            pl.BlockSpec((None, 1, D_MODEL), lambda i: (layer, 0, 0)),
            _mod_spec(scale_chunk, tm, D_MODEL, 0, None),
            _mod_spec(shift_chunk, tm, D_MODEL, 0, None),
        ],
        out_specs=pl.BlockSpec((tm, D_MODEL), lambda i: (i, 0)),
        out_shape=jax.ShapeDtypeStruct((M_ALL, D_MODEL), BF16),
        compiler_params=_params(1),
        name="norm_modulate",
    )(x_ctx, x_lat, gains.reshape(DEPTH, 1, D_MODEL), mod, mod)


def _low_lanes(rows):
    return lax.broadcasted_iota(jnp.int32, (rows, LANES), 1) < HALF_LANES


def _cast_pair_order(w_ref, w_s):
    low = _low_lanes(w_ref.shape[0])
    per_pair = 2 * WIN_GROUP * WIN_HEAD_DIM // LANES
    for p in range(w_ref.shape[1] // (per_pair * LANES)):
        c = [w_ref[:, (per_pair * p + t) * LANES:(per_pair * p + t + 1) * LANES] for t in range(per_pair)]
        r = [pltpu.roll(x, HALF_LANES, 1) for x in c]
        moved = [jnp.where(low, c[0], r[2]), jnp.where(low, r[0], c[2]),
                 jnp.where(low, c[1], r[3]), jnp.where(low, r[1], c[3])]
        for t, x in enumerate(moved):
            w_s[:, (per_pair * p + t) * LANES:(per_pair * p + t + 1) * LANES] = x.astype(BF16)


PROJ_SPLIT = 4


def _proj_body(*refs, rope_lo, rope_hi, n_ctx_tiles, pair_order):
    has_rope = rope_hi > rope_lo
    if has_rope:
        lhs_ref, w_ref, cos_ref, sin_ref, o_ref, w_s = refs
    else:
        lhs_ref, w_ref, o_ref, w_s = refs
    j = pl.program_id(0)
    i = pl.program_id(1)

    @pl.when(i == 0)
    def _():
        if pair_order:
            _cast_pair_order(w_ref, w_s)
        else:
            w_s[...] = w_ref[...].astype(BF16)

    def run(with_rope):
        sub = lhs_ref.shape[0] // PROJ_SPLIT
        pending = None
        for s in range(PROJ_SPLIT + 1):
            if s < PROJ_SPLIT:
                rows = slice(s * sub, (s + 1) * sub)
                cur = (rows, _dot(lhs_ref[rows, :].astype(BF16), w_s[...]))
            if pending is not None:
                prow, acc = pending
                if with_rope:
                    acc = _rope(acc, cos_ref[prow, :], sin_ref[prow, :])
                o_ref[prow, :] = acc.astype(o_ref.dtype)
            pending = cur if s < PROJ_SPLIT else None

    if not has_rope:
        run(False)
    else:
        roped = jnp.logical_and(jnp.logical_and(j >= rope_lo, j < rope_hi), i >= n_ctx_tiles)
        pl.when(roped)(lambda: run(True))
        pl.when(jnp.logical_not(roped))(lambda: run(False))


def _proj_call(lhs, w, layer, col_off, n_cols, out_dtype, *, tm, tn, rope=None, tables=None,
               pair_order=False, name):
    m, k = lhs.shape
    n_tiles = n_cols // tn
    off = col_off // tn
    rope_lo, rope_hi = rope if rope is not None else (0, 0)
    if w.ndim == 3:
        w_spec = pl.BlockSpec((None, k, tn), lambda j, i: (layer, 0, off + j))
    else:
        w_spec = pl.BlockSpec((k, tn), lambda j, i: (0, off + j))
    in_specs = [pl.BlockSpec((tm, k), lambda j, i: (i, 0)), w_spec]
    args = [lhs, w]
    if rope_hi > rope_lo:
        in_specs += [pl.BlockSpec((tm, LANES), lambda j, i: (i, 0))] * 2
        args += list(tables)
    return pl.pallas_call(
        functools.partial(_proj_body, rope_lo=rope_lo, rope_hi=rope_hi, n_ctx_tiles=N_CTX // tm,
                          pair_order=pair_order),
        grid=(n_tiles, m // tm),
        in_specs=in_specs,
        out_specs=pl.BlockSpec((tm, tn), lambda j, i: (i, j)),
        out_shape=jax.ShapeDtypeStruct((m, n_cols), out_dtype),
        scratch_shapes=[pltpu.VMEM((k, tn), BF16)],
        compiler_params=_params(2),
        name=name,
    )(*args)


ROW_BLOCK = WIN_HEAD_DIM


def _resid_norm_body(*refs, n_parts, n_x, ctx_tiles, cast_w, row_perm, final):
    lhs_refs = refs[:n_parts]
    w_ref = refs[n_parts]
    x_refs = refs[n_parts + 1:n_parts + 1 + n_x]
    n_mod = 2 if final else 4
    mods = refs[n_parts + 1 + n_x:n_parts + 1 + n_x + n_mod]
    outs = refs[n_parts + 1 + n_x + n_mod:n_parts + 1 + n_x + n_mod + 2]
    scratch = refs[n_parts + 1 + n_x + n_mod + 2:]
    gate_ref, g_ref = mods[:2]
    i = pl.program_id(0)
    if cast_w:
        w = scratch[0]

        @pl.when(i == 0)
        def _():
            if row_perm is None:
                w[...] = w_ref[...].astype(BF16)
            else:
                for new, old in enumerate(row_perm):
                    w[new * ROW_BLOCK:(new + 1) * ROW_BLOCK, :] = (
                        w_ref[old * ROW_BLOCK:(old + 1) * ROW_BLOCK, :].astype(BF16))
    else:
        w = w_ref

    def step(lhs_ref, x_ref, y_ref):
        xn = x_ref[...] + gate_ref[...] * _dot(lhs_ref[...], w[...])
        y = _rms(xn, g_ref[...])
        if final:
            y_ref[...] = y
        else:
            outs[0][...] = xn
            outs[1][...] = (y * (1.0 + mods[2][...]) + mods[3][...]).astype(outs[1].dtype)

    if n_parts == 1 and n_x == 1 and not final:
        step(lhs_refs[0], x_refs[0], None)
    else:
        pl.when(i < ctx_tiles)(lambda: step(lhs_refs[0], x_refs[0], outs[0]))
        pl.when(i >= ctx_tiles)(lambda: step(lhs_refs[-1], x_refs[-1], outs[1]))


def _resid_norm_call(lhs_parts, w, w_layer, x_parts, mod, gate_chunk, gains, gain_layer, next_mod, *,
                     tm, row_perm=None, name):
    k = lhs_parts[0].shape[1]
    n_parts = len(lhs_parts)
    n_x = len(x_parts)
    ctx_tiles = N_CTX // tm
    final = next_mod is None
    cast_w = w.dtype != BF16
    if w.ndim == 3:
        w_spec = pl.BlockSpec((None, k, D_MODEL), lambda i: (w_layer, 0, 0), pipeline_mode=pl.Buffered(1))
    else:
        w_spec = pl.BlockSpec((k, D_MODEL), lambda i: (0, 0))
    in_specs = _part_specs(n_parts, tm, k) + [w_spec] + _part_specs(n_x, tm, D_MODEL) + [
        _mod_spec(gate_chunk, tm, D_MODEL, 0, None),
        pl.BlockSpec((None, 1, D_MODEL), lambda i: (gain_layer, 0, 0)),
    ]
    args = list(lhs_parts) + [w] + list(x_parts) + [mod, gains]
    if final:
        out_specs = _part_specs(2, tm, D_MODEL)
        out_shape = [jax.ShapeDtypeStruct((N_CTX, D_MODEL), F32), jax.ShapeDtypeStruct((N_LAT, D_MODEL), F32)]
    else:
        mod_next, shift_chunk, scale_chunk = next_mod
        in_specs += [_mod_spec(scale_chunk, tm, D_MODEL, 0, None),
                     _mod_spec(shift_chunk, tm, D_MODEL, 0, None)]
        args += [mod_next, mod_next]
        out_specs = _part_specs(1, tm, D_MODEL) * 2
        out_shape = [jax.ShapeDtypeStruct((M_ALL, D_MODEL), F32), jax.ShapeDtypeStruct((M_ALL, D_MODEL), BF16)]
    return pl.pallas_call(
        functools.partial(_resid_norm_body, n_parts=n_parts, n_x=n_x, ctx_tiles=ctx_tiles, cast_w=cast_w,
                          row_perm=row_perm, final=final),
        grid=(M_ALL // tm,),
        in_specs=in_specs,
        out_specs=out_specs,
        out_shape=out_shape,
        scratch_shapes=[pltpu.VMEM((k, D_MODEL), BF16)] if cast_w else [],
        compiler_params=_params(1),
        name=name,
    )(*args)


SWIGLU_SPLIT = 8


ADA_CHUNK = 3 * LANES
ADA_STEPS = 6 * D_MODEL // ADA_CHUNK


def _swiglu_body(*refs, with_ada):
    if with_ada:
        h_ref, wg_ref, wu_ref, wd_ref, ct_ref, aw_ref, ab_ref, o_ref, wdo_ref, mod_ref, wg_s, wu_s, sb_s = refs
    else:
        h_ref, wg_ref, wu_ref, wd_ref, o_ref, wdo_ref, wg_s, wu_s = refs

    @pl.when(pl.program_id(1) == 0)
    def _():
        wg_s[...] = wg_ref[...].astype(BF16)
        wu_s[...] = wu_ref[...].astype(BF16)

    if with_ada:
        @pl.when(jnp.logical_and(pl.program_id(0) == 0, pl.program_id(1) == 0))
        def _():
            sb_s[...] = jax.nn.silu(ct_ref[...]).astype(BF16)

        def ada_piece():
            mod_ref[...] = _dot(sb_s[...], aw_ref[...].astype(BF16)) + ab_ref[...]

        side = [ada_piece]
    else:
        side = []

    sub = h_ref.shape[0] // SWIGLU_SPLIT
    pending = None
    for s in range(SWIGLU_SPLIT + 1):
        if s < SWIGLU_SPLIT:
            rows = slice(s * sub, (s + 1) * sub)
            h = h_ref[rows, :]
            cur = (rows, _dot(h, wg_s[...]), _dot(h, wu_s[...]))
        if pending is not None:
            prow, g, u = pending
            o_ref[prow, :] = (jax.nn.silu(g) * u).astype(o_ref.dtype)
        pending = cur if s < SWIGLU_SPLIT else None
        for piece in side[s::SWIGLU_SPLIT + 1]:
            piece()
    wdo_ref[...] = wd_ref[...].astype(BF16)


def _swiglu_call(h, w_gate, w_up, w_down, layer, ada=None):
    tm, tn = 2048, 512
    n_i = M_ALL // tm
    slab = D_FF // ((D_FF // tn) * n_i)
    step = lambda j, i: j * n_i + i
    in_specs = [
        pl.BlockSpec((tm, D_MODEL), lambda j, i: (i, 0)),
        pl.BlockSpec((None, D_MODEL, tn), lambda j, i: (layer, 0, j)),
        pl.BlockSpec((None, D_MODEL, tn), lambda j, i: (layer, 0, j)),
        pl.BlockSpec((None, slab, D_MODEL), lambda j, i: (layer, step(j, i), 0)),
    ]
    out_specs = [
        pl.BlockSpec((tm, tn), lambda j, i: (i, j)),
        pl.BlockSpec((slab, D_MODEL), lambda j, i: (step(j, i), 0)),
    ]
    out_shape = [jax.ShapeDtypeStruct((M_ALL, D_FF), BF16), jax.ShapeDtypeStruct((D_FF, D_MODEL), BF16)]
    scratch = [pltpu.VMEM((D_MODEL, tn), BF16), pltpu.VMEM((D_MODEL, tn), BF16)]
    args = [h, w_gate, w_up, w_down]
    if ada is not None:
        assert (D_FF // tn) * n_i >= ADA_STEPS
        cond, ada_w, ada_b = ada
        chunk = lambda j, i: jnp.minimum(step(j, i), ADA_STEPS - 1)
        in_specs += [
            pl.BlockSpec((MOD_ROWS, D_MODEL), lambda j, i: (0, 0)),
            pl.BlockSpec((None, D_MODEL, ADA_CHUNK), lambda j, i: (layer + 1, 0, chunk(j, i))),
            pl.BlockSpec((None, 1, ADA_CHUNK), lambda j, i: (layer + 1, 0, chunk(j, i))),
        ]
        out_specs.append(pl.BlockSpec((MOD_ROWS, ADA_CHUNK), lambda j, i: (0, chunk(j, i))))
        out_shape.append(jax.ShapeDtypeStruct((MOD_ROWS, 6 * D_MODEL), F32))
        scratch.append(pltpu.VMEM((MOD_ROWS, D_MODEL), BF16))
        args += [cond, ada_w, ada_b.reshape(DEPTH, 1, 6 * D_MODEL)]
    return pl.pallas_call(
        functools.partial(_swiglu_body, with_ada=ada is not None),
        grid=(D_FF // tn, n_i),
        in_specs=in_specs,
        out_specs=out_specs,
        out_shape=out_shape,
        scratch_shapes=scratch,
        compiler_params=_params(2, vmem=VMEM_LIMIT_LARGE),
        name="ffn_gate_up",
    )(*args)


def _mla_in_body(h_ref, w_ref, qn_ref, kvn_ref, cos_ref, sin_ref, cq_ref, ckv_ref, kpe_ref, w_s):
    @pl.when(pl.program_id(0) == 0)
    def _():
        w_s[...] = w_ref[...].astype(BF16)

    def epilogue(rows, acc):
        cq = acc[:, :Q_LORA_RANK]
        ckv = acc[:, Q_LORA_RANK:Q_LORA_RANK + KV_LORA_RANK]
        kpe = acc[:, Q_LORA_RANK + KV_LORA_RANK:]
        kpe = jnp.concatenate([kpe, kpe], axis=1)
        cq_ref[rows, :] = _rms(cq, qn_ref[...]).astype(cq_ref.dtype)
        ckv_ref[rows, :] = _rms(ckv, kvn_ref[...])
        kpe_ref[rows, :] = _rope(kpe, cos_ref[rows, :], sin_ref[rows, :])

    sub = h_ref.shape[0] // PROJ_SPLIT
    pending = None
    for s in range(PROJ_SPLIT + 1):
        if s < PROJ_SPLIT:
            rows = slice(s * sub, (s + 1) * sub)
            cur = (rows, _dot(h_ref[rows, :], w_s[...]))
        if pending is not None:
            epilogue(*pending)
        pending = cur if s < PROJ_SPLIT else None


def _mla_in_call(h, w_in, layer, q_norm, kv_norm, tables):
    tm = 1024
    row = lambda i: (i, 0)
    fixed = lambda i: (0, 0)
    n_in = Q_LORA_RANK + KV_LORA_RANK + QK_ROPE_DIM
    return pl.pallas_call(
        _mla_in_body,
        grid=(M_ALL // tm,),
        in_specs=[
            pl.BlockSpec((tm, D_MODEL), row),
            pl.BlockSpec((None, D_MODEL, n_in), lambda i: (layer, 0, 0), pipeline_mode=pl.Buffered(1)),
            pl.BlockSpec((1, Q_LORA_RANK), fixed),
            pl.BlockSpec((1, KV_LORA_RANK), fixed),
            pl.BlockSpec((tm, LANES), row),
            pl.BlockSpec((tm, LANES), row),
        ],
        out_specs=[
            pl.BlockSpec((tm, Q_LORA_RANK), row),
            pl.BlockSpec((tm, KV_LORA_RANK), row),
            pl.BlockSpec((tm, LANES), row),
        ],
        out_shape=[
            jax.ShapeDtypeStruct((M_ALL, Q_LORA_RANK), BF16),
            jax.ShapeDtypeStruct((M_ALL, KV_LORA_RANK), F32),
            jax.ShapeDtypeStruct((M_ALL, LANES), F32),
        ],
        scratch_shapes=[pltpu.VMEM((D_MODEL, n_in), BF16)],
        compiler_params=_params(1),
        name="mla_in_proj",
    )(h, w_in, q_norm.reshape(1, -1), kv_norm.reshape(1, -1), *tables)


def _mla_expand_body(cq_ref, ckv_ref, wq_ref, wkv_ref, cos_ref, sin_ref, q_ref, kvx_ref, wq_s, wkv_s, *,
                     n_ctx_tiles, nope_w):
    i = pl.program_id(0)

    @pl.when(i == 0)
    def _():
        wq_s[...] = wq_ref[...].astype(BF16)
        wkv_s[...] = wkv_ref[...].astype(BF16)

    def run(with_rope):
        sub = cq_ref.shape[0] // PROJ_SPLIT
        pending = None
        for s in range(PROJ_SPLIT + 1):
            if s < PROJ_SPLIT:
                rows = slice(s * sub, (s + 1) * sub)
                cur = (rows, _dot(cq_ref[rows, :], wq_s[...]),
                       _dot(ckv_ref[rows, :].astype(BF16), wkv_s[...]))
            if pending is not None:
                prow, q, kvx = pending
                q_ref[prow, :nope_w] = q[:, :nope_w].astype(q_ref.dtype)
                qp = q[:, nope_w:]
                if with_rope:
                    qp = _rope(qp, cos_ref[prow, :], sin_ref[prow, :])
                q_ref[prow, nope_w:] = qp.astype(q_ref.dtype)
                kvx_ref[prow, :] = kvx.astype(kvx_ref.dtype)
            pending = cur if s < PROJ_SPLIT else None

    pl.when(i >= n_ctx_tiles)(lambda: run(True))
    pl.when(i < n_ctx_tiles)(lambda: run(False))


def _mla_expand_call(cq, ckv, wq, w_kv_b, layer, tables):
    tm = 512
    n_q = wq.shape[1]
    n_kv = w_kv_b.shape[2]
    row = lambda i: (i, 0)
    return pl.pallas_call(
        functools.partial(_mla_expand_body, n_ctx_tiles=N_CTX // tm, nope_w=MLA_HEADS * QK_NOPE_DIM),
        grid=(M_ALL // tm,),
        in_specs=[
            pl.BlockSpec((tm, Q_LORA_RANK), row),
            pl.BlockSpec((tm, KV_LORA_RANK), row),
            pl.BlockSpec((Q_LORA_RANK, n_q), lambda i: (0, 0)),
            pl.BlockSpec((None, KV_LORA_RANK, n_kv), lambda i: (layer, 0, 0), pipeline_mode=pl.Buffered(1)),
            pl.BlockSpec((tm, LANES), row),
            pl.BlockSpec((tm, LANES), row),
        ],
        out_specs=[pl.BlockSpec((tm, n_q), row), pl.BlockSpec((tm, n_kv), row)],
        out_shape=[jax.ShapeDtypeStruct((M_ALL, n_q), BF16), jax.ShapeDtypeStruct((M_ALL, n_kv), BF16)],
        scratch_shapes=[pltpu.VMEM((Q_LORA_RANK, n_q), BF16), pltpu.VMEM((KV_LORA_RANK, n_kv), BF16)],
        compiler_params=_params(1),
        name="mla_expand",
    )(cq, ckv, wq, w_kv_b, *tables)


def _win_heads(q_ref, o_ref, sink_ref, layer, k_chunks, v_chunks, blocks, nq, pipelined):
    low_q = _low_lanes(nq)
    qlo = jnp.where(low_q, WIN_SCALE, 0.0).astype(BF16)
    qhi = jnp.where(low_q, 0.0, WIN_SCALE).astype(BF16)
    heads_per_pair = 2 * WIN_GROUP

    def scores(p):
        kc = k_chunks[p].astype(BF16)
        qcs = [q_ref[:, (WIN_GROUP * p + t) * LANES:(WIN_GROUP * p + t + 1) * LANES] for t in range(WIN_GROUP)]
        qs = jnp.concatenate([qc * qlo for qc in qcs] + [qc * qhi for qc in qcs], axis=0)
        return _dot_nt(qs, kc)

    def softmax(p, lg):
        probs, invs = [], []
        for u in range(heads_per_pair):
            sk = sink_ref[layer, heads_per_pair * p + u]
            l = lg[u * nq:(u + 1) * nq]
            cols = [l[:, a:a + LANES] if msk is None else jnp.where(msk, l[:, a:a + LANES], NEG)
                    for a, msk in blocks]
            mx = cols[0]
            for col in cols[1:]:
                mx = jnp.maximum(mx, col)
            m = jnp.maximum(jnp.max(mx, axis=-1, keepdims=True), sk)
            es = [jnp.exp(col - m) for col in cols]
            tot = es[0]
            for e in es[1:]:
                tot = tot + e
            invs.append(1.0 / (jnp.sum(tot, axis=-1, keepdims=True) + jnp.exp(sk - m)))
            probs.append(jnp.concatenate([e.astype(BF16) for e in es], axis=1))
        return jnp.concatenate(probs, axis=0), invs

    def values(p, probs, invs):
        o = _dot(probs, v_chunks[p].astype(BF16))
        on = [o[u * nq:(u + 1) * nq] * invs[u] for u in range(heads_per_pair)]
        for t in range(WIN_GROUP):
            c = WIN_GROUP * p + t
            o_ref[:, c * LANES:(c + 1) * LANES] = jnp.where(low_q, on[t], on[WIN_GROUP + t]).astype(o_ref.dtype)

    n = WIN_KV_HEADS // 2
    if not pipelined:
        for p in range(n):
            values(p, *softmax(p, scores(p)))
        return
    sc, pr = {}, {}
    for p in range(n + 2):
        if p < n:
            sc[p] = scores(p)
        if 0 <= p - 1 < n:
            pr[p - 1] = softmax(p - 1, sc.pop(p - 1))
        if 0 <= p - 2 < n:
            values(p - 2, *pr.pop(p - 2))


def _win_ctx_body(sink_ref, q_ref, kv_ref, o_ref, *, layer):
    n_pairs = WIN_KD // LANES
    k_chunks = [kv_ref[:, p * LANES:(p + 1) * LANES] for p in range(n_pairs)]
    v_chunks = [kv_ref[:, WIN_KD + p * LANES:WIN_KD + (p + 1) * LANES] for p in range(n_pairs)]
    _win_heads(q_ref, o_ref, sink_ref, layer, k_chunks, v_chunks,
               [(a, None) for a in range(0, SEQ, LANES)], SEQ, False)


def _win_ctx_call(q, kv, sink, layer):
    return pl.pallas_call(
        functools.partial(_win_ctx_body, layer=layer),
        grid=(BATCH,),
        in_specs=[
            pl.BlockSpec(memory_space=pltpu.SMEM),
            pl.BlockSpec((SEQ, WIN_QD), lambda b: (b, 0)),
            pl.BlockSpec((SEQ, 2 * WIN_KD), lambda b: (b, 0)),
        ],
        out_specs=pl.BlockSpec((SEQ, WIN_QD), lambda b: (b, 0)),
        out_shape=jax.ShapeDtypeStruct((N_CTX, WIN_QD), BF16),
        compiler_params=_params(1),
        name="win_attn_context",
    )(sink, q, kv)


def _win_lat_body(sink_ref, q_ref, kvp_ref, kvc_ref, kvn_ref, ck_ref, cv_ref, o_ref, *, layer):
    n = pl.program_id(1)
    n_pairs = WIN_KD // LANES
    k_chunks, v_chunks = [], []
    for p in range(n_pairs):
        ks = slice(p * LANES, (p + 1) * LANES)
        vs = slice(WIN_KD + p * LANES, WIN_KD + (p + 1) * LANES)
        k_chunks.append(jnp.concatenate([kvp_ref[:, ks], kvc_ref[:, ks], kvn_ref[:, ks], ck_ref[:, ks]], axis=0))
        v_chunks.append(jnp.concatenate([kvp_ref[:, vs], kvc_ref[:, vs], kvn_ref[:, vs], cv_ref[:, ks]], axis=0))
    r = lax.broadcasted_iota(jnp.int32, (BLOCK, BLOCK), 0)
    s = lax.broadcasted_iota(jnp.int32, (BLOCK, BLOCK), 1)
    prev_ok = (s - r) >= jnp.where(n >= 1, 0, BLOCK)
    next_ok = (r - s) >= jnp.where(n <= DEC_SEQ // BLOCK - 2, 0, BLOCK)
    blocks = [(0, prev_ok), (BLOCK, None), (2 * BLOCK, next_ok)]
    blocks += [(3 * BLOCK + a, None) for a in range(0, PAST_LEN, LANES)]
    _win_heads(q_ref, o_ref, sink_ref, layer, k_chunks, v_chunks, blocks, BLOCK, True)


def _win_lat_call(q, kv, cache_k, cache_v, sink, layer):
    nb = DEC_SEQ // BLOCK
    base = N_CTX // BLOCK

    def rows(shift):
        return lambda b, n: (base + b * nb + jnp.clip(n + shift, 0, nb - 1), 0)

    cache_spec = pl.BlockSpec((None, None, PAST_LEN, WIN_KD), lambda b, n: (b, layer, 0, 0))
    return pl.pallas_call(
        functools.partial(_win_lat_body, layer=layer),
        grid=(DEC_BATCH, nb),
        in_specs=[
            pl.BlockSpec(memory_space=pltpu.SMEM),
            pl.BlockSpec((BLOCK, WIN_QD), rows(0)),
            pl.BlockSpec((BLOCK, 2 * WIN_KD), rows(-1)),
            pl.BlockSpec((BLOCK, 2 * WIN_KD), rows(0)),
            pl.BlockSpec((BLOCK, 2 * WIN_KD), rows(1)),
            cache_spec,
            cache_spec,
        ],
        out_specs=pl.BlockSpec((BLOCK, WIN_QD), lambda b, n: (b * nb + n, 0)),
        out_shape=jax.ShapeDtypeStruct((N_LAT, WIN_QD), BF16),
        compiler_params=_params(2),
        name="win_attn_latent",
    )(sink, q, kv, kv, kv, cache_k, cache_v)


KV_HEAD_W = QK_NOPE_DIM + V_HEAD_DIM
MLA_NOPE_W = MLA_HEADS * QK_NOPE_DIM
MLA_ROPE_W = MLA_HEADS * QK_ROPE_DIM
MLA_EXP2_SCALE = MLA_SCALE * LOG2E


def _half_masks_bf16(rows):
    low = _low_lanes(rows)
    return jnp.where(low, 1.0, 0.0).astype(BF16), jnp.where(low, 0.0, 1.0).astype(BF16)


def _mla_ctx_body(qn_ref, qp_ref, kvx_ref, kpe_ref, o_ref):
    qlo, qhi = _half_masks_bf16(SEQ)
    kpe2 = kpe_ref[...].astype(BF16)
    for h in range(MLA_HEADS):
        qp = qp_ref[:, (h // 2) * LANES:(h // 2 + 1) * LANES] * (qlo if h % 2 == 0 else qhi)
        qcat = jnp.concatenate([qn_ref[:, h * LANES:(h + 1) * LANES], qp], axis=1)
        kcat = jnp.concatenate([kvx_ref[:, h * KV_HEAD_W:h * KV_HEAD_W + QK_NOPE_DIM], kpe2], axis=1)
        lg = _dot_nt(qcat, kcat)
        m = jnp.max(lg, axis=-1, keepdims=True)
        e = jnp.exp2((lg - m) * MLA_EXP2_SCALE)
        inv = 1.0 / jnp.sum(e, axis=-1, keepdims=True)
        o = _dot(e.astype(BF16), kvx_ref[:, h * KV_HEAD_W + QK_NOPE_DIM:(h + 1) * KV_HEAD_W])
        o_ref[:, h * LANES:(h + 1) * LANES] = (o * inv).astype(o_ref.dtype)


def _mla_ctx_call(q, kvx, kpe):
    return pl.pallas_call(
        _mla_ctx_body,
        grid=(BATCH,),
        in_specs=[
            pl.BlockSpec((SEQ, MLA_NOPE_W), lambda b: (b, 0)),
            pl.BlockSpec((SEQ, MLA_ROPE_W), lambda b: (b, MLA_NOPE_W // MLA_ROPE_W)),
            pl.BlockSpec((SEQ, MLA_HEADS * KV_HEAD_W), lambda b: (b, 0)),
            pl.BlockSpec((SEQ, LANES), lambda b: (b, 0)),
        ],
        out_specs=pl.BlockSpec((SEQ, MLA_HEADS * V_HEAD_DIM), lambda b: (b, 0)),
        out_shape=jax.ShapeDtypeStruct((N_CTX, MLA_HEADS * V_HEAD_DIM), BF16),
        compiler_params=_params(1),
        name="mla_attn_context",
    )(q, q, kvx, kpe)


MLA_QB = 2048
MLA_SUB = 512


def _mla_lat_body(qn_ref, qp_ref, kvl_ref, kvc_ref, kpl_ref, kpc_ref, o_ref):
    qlo, qhi = _half_masks_bf16(MLA_SUB)
    kpl = kpl_ref[...].astype(BF16)
    kpc = kpc_ref[...].astype(BF16)
    qp_pair = qp_ref[...]
    chains = [(t, r) for t in range(2) for r in range(MLA_QB // MLA_SUB)]

    def scores(t, r):
        c0 = t * KV_HEAD_W
        rows = slice(r * MLA_SUB, (r + 1) * MLA_SUB)
        qcat = jnp.concatenate([qn_ref[rows, t * LANES:(t + 1) * LANES],
                                qp_pair[rows] * (qlo if t == 0 else qhi)], axis=1)
        k_lat = jnp.concatenate([kvl_ref[:, c0:c0 + QK_NOPE_DIM], kpl], axis=1)
        k_cache = jnp.concatenate([kvc_ref[:, c0:c0 + QK_NOPE_DIM], kpc], axis=1)
        return _dot_nt(qcat, k_lat), _dot_nt(qcat, k_cache)

    def softmax(l1, l2):
        m = jnp.maximum(jnp.max(l1, axis=-1, keepdims=True), jnp.max(l2, axis=-1, keepdims=True))
        e1 = jnp.exp2((l1 - m) * MLA_EXP2_SCALE)
        e2 = jnp.exp2((l2 - m) * MLA_EXP2_SCALE)
        inv = 1.0 / (jnp.sum(e1, axis=-1, keepdims=True) + jnp.sum(e2, axis=-1, keepdims=True))
        return e1.astype(BF16), e2.astype(BF16), inv

    def values(t, r, p1, p2, inv):
        c0 = t * KV_HEAD_W
        o = (_dot(p1, kvl_ref[:, c0 + QK_NOPE_DIM:c0 + KV_HEAD_W])
             + _dot(p2, kvc_ref[:, c0 + QK_NOPE_DIM:c0 + KV_HEAD_W]))
        o_ref[r * MLA_SUB:(r + 1) * MLA_SUB, t * LANES:(t + 1) * LANES] = (o * inv).astype(o_ref.dtype)

    n = len(chains)
    sc, pr = {}, {}
    for c in range(n + 2):
        if c < n:
            sc[c] = scores(*chains[c])
        if 0 <= c - 1 < n:
            pr[c - 1] = softmax(*sc.pop(c - 1))
        if 0 <= c - 2 < n:
            values(*chains[c - 2], *pr.pop(c - 2))


def _mla_lat_call(q, kvx, kvx_cache, kpe, kpe_cache):
    n_qb = DEC_SEQ // MLA_QB
    n_pairs = MLA_HEADS // 2
    lat_blk = N_CTX // DEC_SEQ
    return pl.pallas_call(
        _mla_lat_body,
        grid=(DEC_BATCH, n_pairs, n_qb),
        in_specs=[
            pl.BlockSpec((MLA_QB, 2 * QK_NOPE_DIM), lambda b, hp, qb: (N_CTX // MLA_QB + b * n_qb + qb, hp)),
            pl.BlockSpec((MLA_QB, LANES), lambda b, hp, qb: (N_CTX // MLA_QB + b * n_qb + qb, MLA_NOPE_W // LANES + hp)),
            pl.BlockSpec((DEC_SEQ, 2 * KV_HEAD_W), lambda b, hp, qb: (lat_blk + b, hp)),
            pl.BlockSpec((PAST_LEN, 2 * KV_HEAD_W), lambda b, hp, qb: (b, hp)),
            pl.BlockSpec((DEC_SEQ, LANES), lambda b, hp, qb: (lat_blk + b, 0)),
            pl.BlockSpec((None, PAST_LEN, LANES), lambda b, hp, qb: (b, 0, 0)),
        ],
        out_specs=pl.BlockSpec((MLA_QB, 2 * V_HEAD_DIM), lambda b, hp, qb: (b * n_qb + qb, hp)),
        out_shape=jax.ShapeDtypeStruct((N_LAT, MLA_HEADS * V_HEAD_DIM), BF16),
        compiler_params=_params(3),
        name="mla_attn_latent",
    )(q, q, kvx, kvx_cache, kpe, kpe_cache)


def _pack_body(*refs, n_layers, col_ranges):
    ins, outs = refs[:-len(col_ranges)], refs[-len(col_ranges):]
    for o_ref, (src, lo, hi) in zip(outs, col_ranges):
        for l in range(n_layers):
            o_ref[0, l] = ins[src * n_layers + l][:, lo:hi]


def _pack_call(sources, col_ranges, name):
    n_layers = len(sources[0])
    flat = [a for src in sources for a in src]
    return pl.pallas_call(
        functools.partial(_pack_body, n_layers=n_layers, col_ranges=col_ranges),
        grid=(BATCH,),
        in_specs=[pl.BlockSpec((SEQ, a.shape[1]), lambda b: (b, 0)) for a in flat],
        out_specs=[pl.BlockSpec((1, n_layers, SEQ, hi - lo), lambda b: (b, 0, 0, 0)) for _, lo, hi in col_ranges],
        out_shape=[jax.ShapeDtypeStruct((BATCH, n_layers, SEQ, hi - lo), F32) for _, lo, hi in col_ranges],
        compiler_params=_params(1),
        name=name,
    )(*flat)


def _rope_tables():
    t = jnp.arange(DEC_SEQ, dtype=jnp.int32)
    rows, cols = t // GRID_W, t % GRID_W
    half = WIN_HEAD_DIM // 4
    freqs = ROPE_BASE ** (-jnp.arange(half, dtype=F32) / half)
    ang_r = rows.astype(F32)[:, None] * freqs[None, :]
    ang_c = cols.astype(F32)[:, None] * freqs[None, :]
    cr, sr, cc, sc = jnp.cos(ang_r), jnp.sin(ang_r), jnp.cos(ang_c), jnp.sin(ang_c)
    c64 = jnp.concatenate([cr, cr, cc, cc], axis=-1)
    s64 = jnp.concatenate([-sr, sr, -sc, sc], axis=-1)
    c_lat = jnp.tile(c64, (DEC_BATCH, LANES // WIN_HEAD_DIM))
    s_lat = jnp.tile(s64, (DEC_BATCH, LANES // WIN_HEAD_DIM))
    cos_t = jnp.concatenate([jnp.ones((N_CTX, LANES), F32), c_lat], axis=0)
    sin_t = jnp.concatenate([jnp.zeros((N_CTX, LANES), F32), s_lat], axis=0)
    return cos_t, sin_t


def kernel(x_prompt, x_sample, cache_win_k, cache_win_v, cache_mla_ckv, cache_mla_kpe, c, c_ctx, ada_w, ada_b, norm_mix, norm_ffn, win_w_qkv, win_w_o, win_sink, mla_w_in, mla_q_norm, mla_w_q_b, mla_kv_norm, mla_w_kv_b, mla_w_o, ffn_w_gate, ffn_w_up, ffn_w_down, norm_final):
    x = [x_prompt.reshape(N_CTX, D_MODEL), x_sample.reshape(N_LAT, D_MODEL)]
    cond = jnp.concatenate([c_ctx[None, :], c, jnp.zeros((MOD_ROWS - 1 - DEC_BATCH, D_MODEL), F32)], axis=0)
    mods = [_ada_call(cond, ada_w, ada_b, 0).reshape(MOD_ROWS, 1, 6 * D_MODEL)]
    tables = _rope_tables()
    cache_k = cache_win_k.reshape(DEC_BATCH, N_WIN_LAYERS, PAST_LEN, WIN_KD)
    cache_v = cache_win_v.reshape(DEC_BATCH, N_WIN_LAYERS, PAST_LEN, WIN_KD)

    kv_layers, ckv_layers, kpe_layers = [], [], []
    mix_gains = norm_mix.reshape(DEPTH, 1, D_MODEL)
    ffn_gains = norm_ffn.reshape(DEPTH, 1, D_MODEL)
    win_row_perm = [8 * p + 4 * s + t for p in range(WIN_KV_HEADS // 2) for t in range(WIN_GROUP) for s in range(2)]
    h = _modulate_call(x[0], x[1], norm_mix, 0, mods[0], 0, 1)
    for layer in range(DEPTH):
        j = layer // 2
        if layer % 2 == 0:
            q = _proj_call(h, win_w_qkv, j, 0, WIN_QD, BF16, tm=1024, tn=1024,
                           rope=(0, WIN_QD // 1024), tables=tables, pair_order=True, name="win_q_proj")
            kv = _proj_call(h, win_w_qkv, j, WIN_QD, 2 * WIN_KD, F32, tm=1024, tn=512,
                            rope=(0, 1), tables=tables, name="win_kv_proj")
            o_ctx = _win_ctx_call(q, kv, win_sink, j)
            o_lat = _win_lat_call(q, kv, cache_k, cache_v, win_sink, j)
            w_o, row_perm = win_w_o, win_row_perm
            kv_layers.append(kv)
        else:
            cq, ckv, kpe = _mla_in_call(h, mla_w_in, j, mla_q_norm[j], mla_kv_norm[j], tables)
            wq = mla_w_q_b[j].reshape(Q_LORA_RANK, MLA_HEADS, QK_NOPE_DIM + QK_ROPE_DIM)
            wq = jnp.concatenate([wq[:, :, :QK_NOPE_DIM].reshape(Q_LORA_RANK, MLA_NOPE_W),
                                  wq[:, :, QK_NOPE_DIM:].reshape(Q_LORA_RANK, MLA_ROPE_W)], axis=1)
            q, kvx = _mla_expand_call(cq, ckv, wq, mla_w_kv_b, j, tables)
            cache_ckv = cache_mla_ckv[:, j].reshape(DEC_BATCH * PAST_LEN, KV_LORA_RANK)
            kvx_cache = _proj_call(cache_ckv, mla_w_kv_b, j, 0, MLA_HEADS * KV_HEAD_W, BF16, tm=512, tn=1024,
                                   name="mla_kv_expand_cache")
            cache_kpe = jnp.tile(cache_mla_kpe[:, j], (1, 1, LANES // QK_ROPE_DIM))
            o_ctx = _mla_ctx_call(q, kvx, kpe)
            o_lat = _mla_lat_call(q, kvx, kvx_cache, kpe, cache_kpe)
            w_o, row_perm = mla_w_o, None
            ckv_layers.append(ckv)
            kpe_layers.append(kpe)
        xn, h = _resid_norm_call([o_ctx, o_lat], w_o, j, x, mods[layer], 2, ffn_gains, layer,
                                 (mods[layer], 3, 4), tm=256, row_perm=row_perm, name="mixer_out_proj")
        if layer + 1 < DEPTH:
            act, w_down, mod_next = _swiglu_call(h, ffn_w_gate, ffn_w_up, ffn_w_down, layer,
                                                 ada=(cond, ada_w, ada_b))
            mods.append(mod_next.reshape(MOD_ROWS, 1, 6 * D_MODEL))
            xn, h = _resid_norm_call([act], w_down, 0, [xn], mods[layer], 5, mix_gains, layer + 1,
                                     (mods[layer + 1], 0, 1), tm=256, name="ffn_down_proj")
            x = [xn]
        else:
            act, w_down = _swiglu_call(h, ffn_w_gate, ffn_w_up, ffn_w_down, layer)
            y_ctx, y_lat = _resid_norm_call([act], w_down, 0, [xn], mods[layer], 5,
                                            norm_final.reshape(1, 1, D_MODEL), 0, None,
                                            tm=256, name="ffn_down_final")

    y_prompt = y_ctx.reshape(BATCH, SEQ, D_MODEL)
    y_sample = y_lat.reshape(DEC_BATCH, DEC_SEQ, D_MODEL)
    new_k, new_v = _pack_call([kv_layers], [(0, 0, WIN_KD), (0, WIN_KD, 2 * WIN_KD)], "pack_win_cache")
    new_ckv, new_kpe = _pack_call([ckv_layers, kpe_layers], [(0, 0, KV_LORA_RANK), (1, 0, QK_ROPE_DIM)],
                                  "pack_mla_cache")
    head_shape = (BATCH, N_WIN_LAYERS, SEQ, WIN_KV_HEADS, WIN_HEAD_DIM)
    return (y_prompt, y_sample, new_k.reshape(head_shape), new_v.reshape(head_shape), new_ckv, new_kpe)
```

```python
import functools
import math

import numpy as np
import jax
import jax.numpy as jnp
from jax import lax
from jax.experimental import pallas as pl
from jax.experimental.pallas import tpu as pltpu

F32 = jnp.float32
BF16 = jnp.bfloat16

D_MODEL = 2048
BATCH = 16
SEQ = 256
DEPTH = 4
DEC_BATCH = 2
DEC_SEQ = 2048
PAST_LEN = 256
GRID_W = 64
N_WIN_LAYERS = 2
N_MLA_LAYERS = 2
WIN_HEADS = 32
WIN_KV_HEADS = 8
WIN_GROUP = WIN_HEADS // WIN_KV_HEADS
WIN_HEAD_DIM = 64
WINDOW = 128
BLOCK = 128
WIN_SCALE = WIN_HEAD_DIM ** -0.5
MLA_HEADS = 16
Q_LORA_RANK = 512
KV_LORA_RANK = 512
QK_NOPE_DIM = 128
QK_ROPE_DIM = 64
V_HEAD_DIM = 128
MLA_SCALE = (QK_NOPE_DIM + QK_ROPE_DIM) ** -0.5
D_FF = 5632
ROPE_BASE = 10000.0
EPS = 1e-6
NEG = float(np.finfo(np.float32).min)
LOG2E = math.log2(math.e)

N_CTX = BATCH * SEQ
N_LAT = DEC_BATCH * DEC_SEQ
M_ALL = N_CTX + N_LAT
MOD_ROWS = 16
LANES = 128
HALF_LANES = LANES // 2
WIN_QD = WIN_HEADS * WIN_HEAD_DIM
WIN_KD = WIN_KV_HEADS * WIN_HEAD_DIM
VMEM_LIMIT = 56 * 1024 * 1024
VMEM_LIMIT_LARGE = 60 * 1024 * 1024


def _params(n_axes, vmem=VMEM_LIMIT):
    return pltpu.CompilerParams(dimension_semantics=("arbitrary",) * n_axes,
                                vmem_limit_bytes=vmem)


def _group(i, tm):
    start = i * tm
    return jnp.where(start < N_CTX, 0, 1 + (start - N_CTX) // DEC_SEQ)


def _mod_spec(chunk, tm, tn, m_axis, n_axis):
    nb = D_MODEL // tn

    def imap(*ids):
        j = 0 if n_axis is None else ids[n_axis]
        return (_group(ids[m_axis], tm), 0, chunk * nb + j)

    return pl.BlockSpec((None, 1, tn), imap)


def _rms(x, g):
    y = x * lax.rsqrt(jnp.mean(x * x, axis=-1, keepdims=True) + EPS)
    return y * g


def _dot(a, b):
    return jnp.dot(a, b, preferred_element_type=F32)


def _dot_nt(a, b):
    return lax.dot_general(a, b, (((1,), (1,)), ((), ())), preferred_element_type=F32)


def _rope(x, c, s):
    lane = lax.broadcasted_iota(jnp.int32, (x.shape[0], LANES), 1)
    first = (lane & 31) < 16
    outs = []
    for k in range(x.shape[1] // LANES):
        a = x[:, k * LANES:(k + 1) * LANES]
        partner = jnp.where(first, pltpu.roll(a, LANES - 16, 1), pltpu.roll(a, 16, 1))
        outs.append(a * c + partner * s)
    return outs[0] if len(outs) == 1 else jnp.concatenate(outs, axis=1)


def _ada_body(cond_ref, w_ref, b_ref, o_ref):
    s = jax.nn.silu(cond_ref[...])
    o_ref[...] = _dot(s.astype(BF16), w_ref[...].astype(BF16)) + b_ref[...]


def _ada_call(cond, ada_w, ada_b, layer):
    tn = 1024
    n = 6 * D_MODEL
    return pl.pallas_call(
        _ada_body,
        grid=(n // tn,),
        in_specs=[
            pl.BlockSpec((MOD_ROWS, D_MODEL), lambda j: (0, 0)),
            pl.BlockSpec((None, D_MODEL, tn), lambda j: (layer, 0, j)),
            pl.BlockSpec((None, 1, tn), lambda j: (layer, 0, j)),
        ],
        out_specs=pl.BlockSpec((MOD_ROWS, tn), lambda j: (0, j)),
        out_shape=jax.ShapeDtypeStruct((MOD_ROWS, n), F32),
        compiler_params=_params(1),
        name="ada_modulation",
    )(cond, ada_w, ada_b.reshape(DEPTH, 1, n))


def _part_specs(n, tm, width):
    ctx_tiles = N_CTX // tm
    if n == 1:
        return [pl.BlockSpec((tm, width), lambda i: (i, 0))]
    return [pl.BlockSpec((tm, width), lambda i: (jnp.minimum(i, ctx_tiles - 1), 0)),
            pl.BlockSpec((tm, width), lambda i: (jnp.maximum(i - ctx_tiles, 0), 0))]


def _modulate_body(xc_ref, xl_ref, g_ref, sc_ref, sh_ref, o_ref, *, ctx_tiles):
    def run(x_ref):
        y = _rms(x_ref[...], g_ref[...])
        o_ref[...] = (y * (1.0 + sc_ref[...]) + sh_ref[...]).astype(o_ref.dtype)

    i = pl.program_id(0)
    pl.when(i < ctx_tiles)(lambda: run(xc_ref))
    pl.when(i >= ctx_tiles)(lambda: run(xl_ref))


def _modulate_call(x_ctx, x_lat, gains, layer, mod, shift_chunk, scale_chunk):
    tm = 512
    return pl.pallas_call(
        functools.partial(_modulate_body, ctx_tiles=N_CTX // tm),
        grid=(M_ALL // tm,),
        in_specs=_part_specs(2, tm, D_MODEL) + [
            pl.BlockSpec((None, 1, D_MODEL), lambda i: (layer, 0, 0)),
            _mod_spec(scale_chunk, tm, D_MODEL, 0, None),
            _mod_spec(shift_chunk, tm, D_MODEL, 0, None),
        ],
        out_specs=pl.BlockSpec((tm, D_MODEL), lambda i: (i, 0)),
        out_shape=jax.ShapeDtypeStruct((M_ALL, D_MODEL), BF16),
        compiler_params=_params(1),
        name="norm_modulate",
    )(x_ctx, x_lat, gains.reshape(DEPTH, 1, D_MODEL), mod, mod)


def _low_lanes(rows):
    return lax.broadcasted_iota(jnp.int32, (rows, LANES), 1) < HALF_LANES


def _cast_pair_order(w_ref, w_s):
    low = _low_lanes(w_ref.shape[0])
    per_pair = 2 * WIN_GROUP * WIN_HEAD_DIM // LANES
    for p in range(w_ref.shape[1] // (per_pair * LANES)):
        c = [w_ref[:, (per_pair * p + t) * LANES:(per_pair * p + t + 1) * LANES] for t in range(per_pair)]
        r = [pltpu.roll(x, HALF_LANES, 1) for x in c]
        moved = [jnp.where(low, c[0], r[2]), jnp.where(low, r[0], c[2]),
                 jnp.where(low, c[1], r[3]), jnp.where(low, r[1], c[3])]
        for t, x in enumerate(moved):
            w_s[:, (per_pair * p + t) * LANES:(per_pair * p + t + 1) * LANES] = x.astype(BF16)


PROJ_SPLIT = 4


def _proj_body(*refs, rope_lo, rope_hi, n_ctx_tiles, pair_order):
    has_rope = rope_hi > rope_lo
    if has_rope:
        lhs_ref, w_ref, cos_ref, sin_ref, o_ref, w_s = refs
    else:
        lhs_ref, w_ref, o_ref, w_s = refs
    j = pl.program_id(0)
    i = pl.program_id(1)

    @pl.when(i == 0)
    def _():
        if pair_order:
            _cast_pair_order(w_ref, w_s)
        else:
            w_s[...] = w_ref[...].astype(BF16)

    def run(with_rope):
        sub = lhs_ref.shape[0] // PROJ_SPLIT
        pending = None
        for s in range(PROJ_SPLIT + 1):
            if s < PROJ_SPLIT:
                rows = slice(s * sub, (s + 1) * sub)
                cur = (rows, _dot(lhs_ref[rows, :].astype(BF16), w_s[...]))
            if pending is not None:
                prow, acc = pending
                if with_rope:
                    acc = _rope(acc, cos_ref[prow, :], sin_ref[prow, :])
                o_ref[prow, :] = acc.astype(o_ref.dtype)
            pending = cur if s < PROJ_SPLIT else None

    if not has_rope:
        run(False)
    else:
        roped = jnp.logical_and(jnp.logical_and(j >= rope_lo, j < rope_hi), i >= n_ctx_tiles)
        pl.when(roped)(lambda: run(True))
        pl.when(jnp.logical_not(roped))(lambda: run(False))


def _proj_call(lhs, w, layer, col_off, n_cols, out_dtype, *, tm, tn, rope=None, tables=None,
               pair_order=False, name):
    m, k = lhs.shape
    n_tiles = n_cols // tn
    off = col_off // tn
    rope_lo, rope_hi = rope if rope is not None else (0, 0)
    if w.ndim == 3:
        w_spec = pl.BlockSpec((None, k, tn), lambda j, i: (layer, 0, off + j))
    else:
        w_spec = pl.BlockSpec((k, tn), lambda j, i: (0, off + j))
    in_specs = [pl.BlockSpec((tm, k), lambda j, i: (i, 0)), w_spec]
    args = [lhs, w]
    if rope_hi > rope_lo:
        in_specs += [pl.BlockSpec((tm, LANES), lambda j, i: (i, 0))] * 2
        args += list(tables)
    return pl.pallas_call(
        functools.partial(_proj_body, rope_lo=rope_lo, rope_hi=rope_hi, n_ctx_tiles=N_CTX // tm,
                          pair_order=pair_order),
        grid=(n_tiles, m // tm),
        in_specs=in_specs,
        out_specs=pl.BlockSpec((tm, tn), lambda j, i: (i, j)),
        out_shape=jax.ShapeDtypeStruct((m, n_cols), out_dtype),
        scratch_shapes=[pltpu.VMEM((k, tn), BF16)],
        compiler_params=_params(2),
        name=name,
    )(*args)


ROW_BLOCK = WIN_HEAD_DIM


def _resid_norm_body(*refs, n_parts, n_x, ctx_tiles, cast_w, row_perm, final):
    lhs_refs = refs[:n_parts]
    w_ref = refs[n_parts]
    x_refs = refs[n_parts + 1:n_parts + 1 + n_x]
    n_mod = 2 if final else 4
    mods = refs[n_parts + 1 + n_x:n_parts + 1 + n_x + n_mod]
    outs = refs[n_parts + 1 + n_x + n_mod:n_parts + 1 + n_x + n_mod + 2]
    scratch = refs[n_parts + 1 + n_x + n_mod + 2:]
    gate_ref, g_ref = mods[:2]
    i = pl.program_id(0)
    if cast_w:
        w = scratch[0]

        @pl.when(i == 0)
        def _():
            if row_perm is None:
                w[...] = w_ref[...].astype(BF16)
            else:
                for new, old in enumerate(row_perm):
                    w[new * ROW_BLOCK:(new + 1) * ROW_BLOCK, :] = (
                        w_ref[old * ROW_BLOCK:(old + 1) * ROW_BLOCK, :].astype(BF16))
    else:
        w = w_ref

    def step(lhs_ref, x_ref, y_ref):
        xn = x_ref[...] + gate_ref[...] * _dot(lhs_ref[...], w[...])
        y = _rms(xn, g_ref[...])
        if final:
            y_ref[...] = y
        else:
            outs[0][...] = xn
            outs[1][...] = (y * (1.0 + mods[2][...]) + mods[3][...]).astype(outs[1].dtype)

    if n_parts == 1 and n_x == 1 and not final:
        step(lhs_refs[0], x_refs[0], None)
    else:
        pl.when(i < ctx_tiles)(lambda: step(lhs_refs[0], x_refs[0], outs[0]))
        pl.when(i >= ctx_tiles)(lambda: step(lhs_refs[-1], x_refs[-1], outs[1]))


def _resid_norm_call(lhs_parts, w, w_layer, x_parts, mod, gate_chunk, gains, gain_layer, next_mod, *,
                     tm, row_perm=None, name):
    k = lhs_parts[0].shape[1]
    n_parts = len(lhs_parts)
    n_x = len(x_parts)
    ctx_tiles = N_CTX // tm
    final = next_mod is None
    cast_w = w.dtype != BF16
    if w.ndim == 3:
        w_spec = pl.BlockSpec((None, k, D_MODEL), lambda i: (w_layer, 0, 0), pipeline_mode=pl.Buffered(1))
    else:
        w_spec = pl.BlockSpec((k, D_MODEL), lambda i: (0, 0))
    in_specs = _part_specs(n_parts, tm, k) + [w_spec] + _part_specs(n_x, tm, D_MODEL) + [
        _mod_spec(gate_chunk, tm, D_MODEL, 0, None),
        pl.BlockSpec((None, 1, D_MODEL), lambda i: (gain_layer, 0, 0)),
    ]
    args = list(lhs_parts) + [w] + list(x_parts) + [mod, gains]
    if final:
        out_specs = _part_specs(2, tm, D_MODEL)
        out_shape = [jax.ShapeDtypeStruct((N_CTX, D_MODEL), F32), jax.ShapeDtypeStruct((N_LAT, D_MODEL), F32)]
    else:
        mod_next, shift_chunk, scale_chunk = next_mod
        in_specs += [_mod_spec(scale_chunk, tm, D_MODEL, 0, None),
                     _mod_spec(shift_chunk, tm, D_MODEL, 0, None)]
        args += [mod_next, mod_next]
        out_specs = _part_specs(1, tm, D_MODEL) * 2
        out_shape = [jax.ShapeDtypeStruct((M_ALL, D_MODEL), F32), jax.ShapeDtypeStruct((M_ALL, D_MODEL), BF16)]
    return pl.pallas_call(
        functools.partial(_resid_norm_body, n_parts=n_parts, n_x=n_x, ctx_tiles=ctx_tiles, cast_w=cast_w,
                          row_perm=row_perm, final=final),
        grid=(M_ALL // tm,),
        in_specs=in_specs,
        out_specs=out_specs,
        out_shape=out_shape,
        scratch_shapes=[pltpu.VMEM((k, D_MODEL), BF16)] if cast_w else [],
        compiler_params=_params(1),
        name=name,
    )(*args)


SWIGLU_SPLIT = 8


ADA_CHUNK = 3 * LANES
ADA_STEPS = 6 * D_MODEL // ADA_CHUNK


def _swiglu_body(*refs, with_ada):
    if with_ada:
        h_ref, wg_ref, wu_ref, wd_ref, ct_ref, aw_ref, ab_ref, o_ref, wdo_ref, mod_ref, wg_s, wu_s, sb_s = refs
    else:
        h_ref, wg_ref, wu_ref, wd_ref, o_ref, wdo_ref, wg_s, wu_s = refs

    @pl.when(pl.program_id(1) == 0)
    def _():
        wg_s[...] = wg_ref[...].astype(BF16)
        wu_s[...] = wu_ref[...].astype(BF16)

    if with_ada:
        @pl.when(jnp.logical_and(pl.program_id(0) == 0, pl.program_id(1) == 0))
        def _():
            sb_s[...] = jax.nn.silu(ct_ref[...]).astype(BF16)

        def ada_piece():
            mod_ref[...] = _dot(sb_s[...], aw_ref[...].astype(BF16)) + ab_ref[...]

        side = [ada_piece]
    else:
        side = []

    sub = h_ref.shape[0] // SWIGLU_SPLIT
    pending = None
    for s in range(SWIGLU_SPLIT + 1):
        if s < SWIGLU_SPLIT:
            rows = slice(s * sub, (s + 1) * sub)
            h = h_ref[rows, :]
            cur = (rows, _dot(h, wg_s[...]), _dot(h, wu_s[...]))
        if pending is not None:
            prow, g, u = pending
            o_ref[prow, :] = (jax.nn.silu(g) * u).astype(o_ref.dtype)
        pending = cur if s < SWIGLU_SPLIT else None
        for piece in side[s::SWIGLU_SPLIT + 1]:
            piece()
    wdo_ref[...] = wd_ref[...].astype(BF16)


def _swiglu_call(h, w_gate, w_up, w_down, layer, ada=None):
    tm, tn = 2048, 512
    n_i = M_ALL // tm
    slab = D_FF // ((D_FF // tn) * n_i)
    step = lambda j, i: j * n_i + i
    in_specs = [
        pl.BlockSpec((tm, D_MODEL), lambda j, i: (i, 0)),
        pl.BlockSpec((None, D_MODEL, tn), lambda j, i: (layer, 0, j)),
        pl.BlockSpec((None, D_MODEL, tn), lambda j, i: (layer, 0, j)),
        pl.BlockSpec((None, slab, D_MODEL), lambda j, i: (layer, step(j, i), 0)),
    ]
    out_specs = [
        pl.BlockSpec((tm, tn), lambda j, i: (i, j)),
        pl.BlockSpec((slab, D_MODEL), lambda j, i: (step(j, i), 0)),
    ]
    out_shape = [jax.ShapeDtypeStruct((M_ALL, D_FF), BF16), jax.ShapeDtypeStruct((D_FF, D_MODEL), BF16)]
    scratch = [pltpu.VMEM((D_MODEL, tn), BF16), pltpu.VMEM((D_MODEL, tn), BF16)]
    args = [h, w_gate, w_up, w_down]
    if ada is not None:
        assert (D_FF // tn) * n_i >= ADA_STEPS
        cond, ada_w, ada_b = ada
        chunk = lambda j, i: jnp.minimum(step(j, i), ADA_STEPS - 1)
        in_specs += [
            pl.BlockSpec((MOD_ROWS, D_MODEL), lambda j, i: (0, 0)),
            pl.BlockSpec((None, D_MODEL, ADA_CHUNK), lambda j, i: (layer + 1, 0, chunk(j, i))),
            pl.BlockSpec((None, 1, ADA_CHUNK), lambda j, i: (layer + 1, 0, chunk(j, i))),
        ]
        out_specs.append(pl.BlockSpec((MOD_ROWS, ADA_CHUNK), lambda j, i: (0, chunk(j, i))))
        out_shape.append(jax.ShapeDtypeStruct((MOD_ROWS, 6 * D_MODEL), F32))
        scratch.append(pltpu.VMEM((MOD_ROWS, D_MODEL), BF16))
        args += [cond, ada_w, ada_b.reshape(DEPTH, 1, 6 * D_MODEL)]
    return pl.pallas_call(
        functools.partial(_swiglu_body, with_ada=ada is not None),
        grid=(D_FF // tn, n_i),
        in_specs=in_specs,
        out_specs=out_specs,
        out_shape=out_shape,
        scratch_shapes=scratch,
        compiler_params=_params(2, vmem=VMEM_LIMIT_LARGE),
        name="ffn_gate_up",
    )(*args)


def _mla_in_body(h_ref, w_ref, qn_ref, kvn_ref, cos_ref, sin_ref, cq_ref, ckv_ref, kpe_ref, w_s):
    @pl.when(pl.program_id(0) == 0)
    def _():
        w_s[...] = w_ref[...].astype(BF16)

    def epilogue(rows, acc):
        cq = acc[:, :Q_LORA_RANK]
        ckv = acc[:, Q_LORA_RANK:Q_LORA_RANK + KV_LORA_RANK]
        kpe = acc[:, Q_LORA_RANK + KV_LORA_RANK:]
        kpe = jnp.concatenate([kpe, kpe], axis=1)
        cq_ref[rows, :] = _rms(cq, qn_ref[...]).astype(cq_ref.dtype)
        ckv_ref[rows, :] = _rms(ckv, kvn_ref[...])
        kpe_ref[rows, :] = _rope(kpe, cos_ref[rows, :], sin_ref[rows, :])

    sub = h_ref.shape[0] // PROJ_SPLIT
    pending = None
    for s in range(PROJ_SPLIT + 1):
        if s < PROJ_SPLIT:
            rows = slice(s * sub, (s + 1) * sub)
            cur = (rows, _dot(h_ref[rows, :], w_s[...]))
        if pending is not None:
            epilogue(*pending)
        pending = cur if s < PROJ_SPLIT else None


def _mla_in_call(h, w_in, layer, q_norm, kv_norm, tables):
    tm = 1024
    row = lambda i: (i, 0)
    fixed = lambda i: (0, 0)
    n_in = Q_LORA_RANK + KV_LORA_RANK + QK_ROPE_DIM
    return pl.pallas_call(
        _mla_in_body,
        grid=(M_ALL // tm,),
        in_specs=[
            pl.BlockSpec((tm, D_MODEL), row),
            pl.BlockSpec((None, D_MODEL, n_in), lambda i: (layer, 0, 0), pipeline_mode=pl.Buffered(1)),
            pl.BlockSpec((1, Q_LORA_RANK), fixed),
            pl.BlockSpec((1, KV_LORA_RANK), fixed),
            pl.BlockSpec((tm, LANES), row),
            pl.BlockSpec((tm, LANES), row),
        ],
        out_specs=[
            pl.BlockSpec((tm, Q_LORA_RANK), row),
            pl.BlockSpec((tm, KV_LORA_RANK), row),
            pl.BlockSpec((tm, LANES), row),
        ],
        out_shape=[
            jax.ShapeDtypeStruct((M_ALL, Q_LORA_RANK), BF16),
            jax.ShapeDtypeStruct((M_ALL, KV_LORA_RANK), F32),
            jax.ShapeDtypeStruct((M_ALL, LANES), F32),
        ],
        scratch_shapes=[pltpu.VMEM((D_MODEL, n_in), BF16)],
        compiler_params=_params(1),
        name="mla_in_proj",
    )(h, w_in, q_norm.reshape(1, -1), kv_norm.reshape(1, -1), *tables)


def _mla_expand_body(cq_ref, ckv_ref, wq_ref, wkv_ref, cos_ref, sin_ref, q_ref, kvx_ref, wq_s, wkv_s, *,
                     n_ctx_tiles, nope_w):
    i = pl.program_id(0)

    @pl.when(i == 0)
    def _():
        wq_s[...] = wq_ref[...].astype(BF16)
        wkv_s[...] = wkv_ref[...].astype(BF16)

    def run(with_rope):
        sub = cq_ref.shape[0] // PROJ_SPLIT
        pending = None
        for s in range(PROJ_SPLIT + 1):
            if s < PROJ_SPLIT:
                rows = slice(s * sub, (s + 1) * sub)
                cur = (rows, _dot(cq_ref[rows, :], wq_s[...]),
                       _dot(ckv_ref[rows, :].astype(BF16), wkv_s[...]))
            if pending is not None:
                prow, q, kvx = pending
                q_ref[prow, :nope_w] = q[:, :nope_w].astype(q_ref.dtype)
                qp = q[:, nope_w:]
                if with_rope:
                    qp = _rope(qp, cos_ref[prow, :], sin_ref[prow, :])
                q_ref[prow, nope_w:] = qp.astype(q_ref.dtype)
                kvx_ref[prow, :] = kvx.astype(kvx_ref.dtype)
            pending = cur if s < PROJ_SPLIT else None

    pl.when(i >= n_ctx_tiles)(lambda: run(True))
    pl.when(i < n_ctx_tiles)(lambda: run(False))


def _mla_expand_call(cq, ckv, wq, w_kv_b, layer, tables):
    tm = 512
    n_q = wq.shape[1]
    n_kv = w_kv_b.shape[2]
    row = lambda i: (i, 0)
    return pl.pallas_call(
        functools.partial(_mla_expand_body, n_ctx_tiles=N_CTX // tm, nope_w=MLA_HEADS * QK_NOPE_DIM),
        grid=(M_ALL // tm,),
        in_specs=[
            pl.BlockSpec((tm, Q_LORA_RANK), row),
            pl.BlockSpec((tm, KV_LORA_RANK), row),
            pl.BlockSpec((Q_LORA_RANK, n_q), lambda i: (0, 0)),
            pl.BlockSpec((None, KV_LORA_RANK, n_kv), lambda i: (layer, 0, 0), pipeline_mode=pl.Buffered(1)),
            pl.BlockSpec((tm, LANES), row),
            pl.BlockSpec((tm, LANES), row),
        ],
        out_specs=[pl.BlockSpec((tm, n_q), row), pl.BlockSpec((tm, n_kv), row)],
        out_shape=[jax.ShapeDtypeStruct((M_ALL, n_q), BF16), jax.ShapeDtypeStruct((M_ALL, n_kv), BF16)],
        scratch_shapes=[pltpu.VMEM((Q_LORA_RANK, n_q), BF16), pltpu.VMEM((KV_LORA_RANK, n_kv), BF16)],
        compiler_params=_params(1),
        name="mla_expand",
    )(cq, ckv, wq, w_kv_b, *tables)


def _win_heads(q_ref, o_ref, sink_ref, layer, k_chunks, v_chunks, blocks, nq, pipelined):
    low_q = _low_lanes(nq)
    qlo = jnp.where(low_q, WIN_SCALE, 0.0).astype(BF16)
    qhi = jnp.where(low_q, 0.0, WIN_SCALE).astype(BF16)
    heads_per_pair = 2 * WIN_GROUP

    def scores(p):
        kc = k_chunks[p].astype(BF16)
        qcs = [q_ref[:, (WIN_GROUP * p + t) * LANES:(WIN_GROUP * p + t + 1) * LANES] for t in range(WIN_GROUP)]
        qs = jnp.concatenate([qc * qlo for qc in qcs] + [qc * qhi for qc in qcs], axis=0)
        return _dot_nt(qs, kc)

    def softmax(p, lg):
        probs, invs = [], []
        for u in range(heads_per_pair):
            sk = sink_ref[layer, heads_per_pair * p + u]
            l = lg[u * nq:(u + 1) * nq]
            cols = [l[:, a:a + LANES] if msk is None else jnp.where(msk, l[:, a:a + LANES], NEG)
                    for a, msk in blocks]
            mx = cols[0]
            for col in cols[1:]:
                mx = jnp.maximum(mx, col)
            m = jnp.maximum(jnp.max(mx, axis=-1, keepdims=True), sk)
            es = [jnp.exp(col - m) for col in cols]
            tot = es[0]
            for e in es[1:]:
                tot = tot + e
            invs.append(1.0 / (jnp.sum(tot, axis=-1, keepdims=True) + jnp.exp(sk - m)))
            probs.append(jnp.concatenate([e.astype(BF16) for e in es], axis=1))
        return jnp.concatenate(probs, axis=0), invs

    def values(p, probs, invs):
        o = _dot(probs, v_chunks[p].astype(BF16))
        on = [o[u * nq:(u + 1) * nq] * invs[u] for u in range(heads_per_pair)]
        for t in range(WIN_GROUP):
            c = WIN_GROUP * p + t
            o_ref[:, c * LANES:(c + 1) * LANES] = jnp.where(low_q, on[t], on[WIN_GROUP + t]).astype(o_ref.dtype)

    n = WIN_KV_HEADS // 2
    if not pipelined:
        for p in range(n):
            values(p, *softmax(p, scores(p)))
        return
    sc, pr = {}, {}
    for p in range(n + 2):
        if p < n:
            sc[p] = scores(p)
        if 0 <= p - 1 < n:
            pr[p - 1] = softmax(p - 1, sc.pop(p - 1))
        if 0 <= p - 2 < n:
            values(p - 2, *pr.pop(p - 2))


def _win_ctx_body(sink_ref, q_ref, kv_ref, o_ref, *, layer):
    n_pairs = WIN_KD // LANES
    k_chunks = [kv_ref[:, p * LANES:(p + 1) * LANES] for p in range(n_pairs)]
    v_chunks = [kv_ref[:, WIN_KD + p * LANES:WIN_KD + (p + 1) * LANES] for p in range(n_pairs)]
    _win_heads(q_ref, o_ref, sink_ref, layer, k_chunks, v_chunks,
               [(a, None) for a in range(0, SEQ, LANES)], SEQ, False)


def _win_ctx_call(q, kv, sink, layer):
    return pl.pallas_call(
        functools.partial(_win_ctx_body, layer=layer),
        grid=(BATCH,),
        in_specs=[
            pl.BlockSpec(memory_space=pltpu.SMEM),
            pl.BlockSpec((SEQ, WIN_QD), lambda b: (b, 0)),
            pl.BlockSpec((SEQ, 2 * WIN_KD), lambda b: (b, 0)),
        ],
        out_specs=pl.BlockSpec((SEQ, WIN_QD), lambda b: (b, 0)),
        out_shape=jax.ShapeDtypeStruct((N_CTX, WIN_QD), BF16),
        compiler_params=_params(1),
        name="win_attn_context",
    )(sink, q, kv)


def _win_lat_body(sink_ref, q_ref, kvp_ref, kvc_ref, kvn_ref, ck_ref, cv_ref, o_ref, *, layer):
    n = pl.program_id(1)
    n_pairs = WIN_KD // LANES
    k_chunks, v_chunks = [], []
    for p in range(n_pairs):
        ks = slice(p * LANES, (p + 1) * LANES)
        vs = slice(WIN_KD + p * LANES, WIN_KD + (p + 1) * LANES)
        k_chunks.append(jnp.concatenate([kvp_ref[:, ks], kvc_ref[:, ks], kvn_ref[:, ks], ck_ref[:, ks]], axis=0))
        v_chunks.append(jnp.concatenate([kvp_ref[:, vs], kvc_ref[:, vs], kvn_ref[:, vs], cv_ref[:, ks]], axis=0))
    r = lax.broadcasted_iota(jnp.int32, (BLOCK, BLOCK), 0)
    s = lax.broadcasted_iota(jnp.int32, (BLOCK, BLOCK), 1)
    prev_ok = (s - r) >= jnp.where(n >= 1, 0, BLOCK)
    next_ok = (r - s) >= jnp.where(n <= DEC_SEQ // BLOCK - 2, 0, BLOCK)
    blocks = [(0, prev_ok), (BLOCK, None), (2 * BLOCK, next_ok)]
    blocks += [(3 * BLOCK + a, None) for a in range(0, PAST_LEN, LANES)]
    _win_heads(q_ref, o_ref, sink_ref, layer, k_chunks, v_chunks, blocks, BLOCK, True)


def _win_lat_call(q, kv, cache_k, cache_v, sink, layer):
    nb = DEC_SEQ // BLOCK
    base = N_CTX // BLOCK

    def rows(shift):
        return lambda b, n: (base + b * nb + jnp.clip(n + shift, 0, nb - 1), 0)

    cache_spec = pl.BlockSpec((None, None, PAST_LEN, WIN_KD), lambda b, n: (b, layer, 0, 0))
    return pl.pallas_call(
        functools.partial(_win_lat_body, layer=layer),
        grid=(DEC_BATCH, nb),
        in_specs=[
            pl.BlockSpec(memory_space=pltpu.SMEM),
            pl.BlockSpec((BLOCK, WIN_QD), rows(0)),
            pl.BlockSpec((BLOCK, 2 * WIN_KD), rows(-1)),
            pl.BlockSpec((BLOCK, 2 * WIN_KD), rows(0)),
            pl.BlockSpec((BLOCK, 2 * WIN_KD), rows(1)),
            cache_spec,
            cache_spec,
        ],
        out_specs=pl.BlockSpec((BLOCK, WIN_QD), lambda b, n: (b * nb + n, 0)),
        out_shape=jax.ShapeDtypeStruct((N_LAT, WIN_QD), BF16),
        compiler_params=_params(2),
        name="win_attn_latent",
    )(sink, q, kv, kv, kv, cache_k, cache_v)


KV_HEAD_W = QK_NOPE_DIM + V_HEAD_DIM
MLA_NOPE_W = MLA_HEADS * QK_NOPE_DIM
MLA_ROPE_W = MLA_HEADS * QK_ROPE_DIM
MLA_EXP2_SCALE = MLA_SCALE * LOG2E


def _half_masks_bf16(rows):
    low = _low_lanes(rows)
    return jnp.where(low, 1.0, 0.0).astype(BF16), jnp.where(low, 0.0, 1.0).astype(BF16)


def _mla_ctx_body(qn_ref, qp_ref, kvx_ref, kpe_ref, o_ref):
    qlo, qhi = _half_masks_bf16(SEQ)
    kpe2 = kpe_ref[...].astype(BF16)
    def scores(h):
        qp = qp_ref[:, (h // 2) * LANES:(h // 2 + 1) * LANES] * (qlo if h % 2 == 0 else qhi)
        qcat = jnp.concatenate([qn_ref[:, h * LANES:(h + 1) * LANES], qp], axis=1)
        kcat = jnp.concatenate([kvx_ref[:, h * KV_HEAD_W:h * KV_HEAD_W + QK_NOPE_DIM], kpe2], axis=1)
        return _dot_nt(qcat, kcat)

    def softmax(lg):
        m = jnp.max(lg, axis=-1, keepdims=True)
        e = jnp.exp2((lg - m) * MLA_EXP2_SCALE)
        return e.astype(BF16), 1.0 / jnp.sum(e, axis=-1, keepdims=True)

    def values(h, p, inv):
        o = _dot(p, kvx_ref[:, h * KV_HEAD_W + QK_NOPE_DIM:(h + 1) * KV_HEAD_W])
        o_ref[:, h * LANES:(h + 1) * LANES] = (o * inv).astype(o_ref.dtype)

    sc, pr = {}, {}
    for h in range(MLA_HEADS + 2):
        if h < MLA_HEADS:
            sc[h] = scores(h)
        if 0 <= h - 1 < MLA_HEADS:
            pr[h - 1] = softmax(sc.pop(h - 1))
        if 0 <= h - 2 < MLA_HEADS:
            values(h - 2, *pr.pop(h - 2))


def _mla_ctx_call(q, kvx, kpe):
    return pl.pallas_call(
        _mla_ctx_body,
        grid=(BATCH,),
        in_specs=[
            pl.BlockSpec((SEQ, MLA_NOPE_W), lambda b: (b, 0)),
            pl.BlockSpec((SEQ, MLA_ROPE_W), lambda b: (b, MLA_NOPE_W // MLA_ROPE_W)),
            pl.BlockSpec((SEQ, MLA_HEADS * KV_HEAD_W), lambda b: (b, 0)),
            pl.BlockSpec((SEQ, LANES), lambda b: (b, 0)),
        ],
        out_specs=pl.BlockSpec((SEQ, MLA_HEADS * V_HEAD_DIM), lambda b: (b, 0)),
        out_shape=jax.ShapeDtypeStruct((N_CTX, MLA_HEADS * V_HEAD_DIM), BF16),
        compiler_params=_params(1),
        name="mla_attn_context",
    )(q, q, kvx, kpe)


MLA_QB = 2048
MLA_SUB = 512


def _mla_lat_body(qn_ref, qp_ref, kvl_ref, cc_ref, wkv_ref, kpl_ref, kpc_ref, o_ref):
    qlo, qhi = _half_masks_bf16(MLA_SUB)
    kvc = _dot(cc_ref[...].astype(BF16), wkv_ref[...].astype(BF16)).astype(BF16)
    kpl = kpl_ref[...].astype(BF16)
    kpc = kpc_ref[...].astype(BF16)
    qp_pair = qp_ref[...]
    chains = [(t, r) for t in range(2) for r in range(MLA_QB // MLA_SUB)]

    def scores(t, r):
        c0 = t * KV_HEAD_W
        rows = slice(r * MLA_SUB, (r + 1) * MLA_SUB)
        qcat = jnp.concatenate([qn_ref[rows, t * LANES:(t + 1) * LANES],
                                qp_pair[rows] * (qlo if t == 0 else qhi)], axis=1)
        k_lat = jnp.concatenate([kvl_ref[:, c0:c0 + QK_NOPE_DIM], kpl], axis=1)
        k_cache = jnp.concatenate([kvc[:, c0:c0 + QK_NOPE_DIM], kpc], axis=1)
        return _dot_nt(qcat, k_lat), _dot_nt(qcat, k_cache)

    def softmax(l1, l2):
        m = jnp.maximum(jnp.max(l1, axis=-1, keepdims=True), jnp.max(l2, axis=-1, keepdims=True))
        e1 = jnp.exp2((l1 - m) * MLA_EXP2_SCALE)
        e2 = jnp.exp2((l2 - m) * MLA_EXP2_SCALE)
        inv = 1.0 / (jnp.sum(e1, axis=-1, keepdims=True) + jnp.sum(e2, axis=-1, keepdims=True))
        return e1.astype(BF16), e2.astype(BF16), inv

    def values(t, r, p1, p2, inv):
        c0 = t * KV_HEAD_W
        o = (_dot(p1, kvl_ref[:, c0 + QK_NOPE_DIM:c0 + KV_HEAD_W])
             + _dot(p2, kvc[:, c0 + QK_NOPE_DIM:c0 + KV_HEAD_W]))
        o_ref[r * MLA_SUB:(r + 1) * MLA_SUB, t * LANES:(t + 1) * LANES] = (o * inv).astype(o_ref.dtype)

    n = len(chains)
    sc, pr = {}, {}
    for c in range(n + 2):
        if c < n:
            sc[c] = scores(*chains[c])
        if 0 <= c - 1 < n:
            pr[c - 1] = softmax(*sc.pop(c - 1))
        if 0 <= c - 2 < n:
            values(*chains[c - 2], *pr.pop(c - 2))


def _mla_lat_call(q, kvx, cache_ckv, w_kv_b, layer, kpe, kpe_cache):
    n_qb = DEC_SEQ // MLA_QB
    n_pairs = MLA_HEADS // 2
    lat_blk = N_CTX // DEC_SEQ
    return pl.pallas_call(
        _mla_lat_body,
        grid=(DEC_BATCH, n_pairs, n_qb),
        in_specs=[
            pl.BlockSpec((MLA_QB, 2 * QK_NOPE_DIM), lambda b, hp, qb: (N_CTX // MLA_QB + b * n_qb + qb, hp)),
            pl.BlockSpec((MLA_QB, LANES), lambda b, hp, qb: (N_CTX // MLA_QB + b * n_qb + qb, MLA_NOPE_W // LANES + hp)),
            pl.BlockSpec((DEC_SEQ, 2 * KV_HEAD_W), lambda b, hp, qb: (lat_blk + b, hp)),
            pl.BlockSpec((None, None, PAST_LEN, KV_LORA_RANK), lambda b, hp, qb: (b, layer, 0, 0)),
            pl.BlockSpec((None, KV_LORA_RANK, 2 * KV_HEAD_W), lambda b, hp, qb: (layer, 0, hp)),
            pl.BlockSpec((DEC_SEQ, LANES), lambda b, hp, qb: (lat_blk + b, 0)),
            pl.BlockSpec((None, PAST_LEN, LANES), lambda b, hp, qb: (b, 0, 0)),
        ],
        out_specs=pl.BlockSpec((MLA_QB, 2 * V_HEAD_DIM), lambda b, hp, qb: (b * n_qb + qb, hp)),
        out_shape=jax.ShapeDtypeStruct((N_LAT, MLA_HEADS * V_HEAD_DIM), BF16),
        compiler_params=_params(3),
        name="mla_attn_latent",
    )(q, q, kvx, cache_ckv, w_kv_b, kpe, kpe_cache)


def _pack_body(*refs, n_layers, col_ranges):
    ins, outs = refs[:-len(col_ranges)], refs[-len(col_ranges):]
    for o_ref, (src, lo, hi) in zip(outs, col_ranges):
        for l in range(n_layers):
            o_ref[0, l] = ins[src * n_layers + l][:, lo:hi]


def _pack_call(sources, col_ranges, name):
    n_layers = len(sources[0])
    flat = [a for src in sources for a in src]
    return pl.pallas_call(
        functools.partial(_pack_body, n_layers=n_layers, col_ranges=col_ranges),
        grid=(BATCH,),
        in_specs=[pl.BlockSpec((SEQ, a.shape[1]), lambda b: (b, 0)) for a in flat],
        out_specs=[pl.BlockSpec((1, n_layers, SEQ, hi - lo), lambda b: (b, 0, 0, 0)) for _, lo, hi in col_ranges],
        out_shape=[jax.ShapeDtypeStruct((BATCH, n_layers, SEQ, hi - lo), F32) for _, lo, hi in col_ranges],
        compiler_params=_params(1),
        name=name,
    )(*flat)


def _rope_tables():
    t = jnp.arange(DEC_SEQ, dtype=jnp.int32)
    rows, cols = t // GRID_W, t % GRID_W
    half = WIN_HEAD_DIM // 4
    freqs = ROPE_BASE ** (-jnp.arange(half, dtype=F32) / half)
    ang_r = rows.astype(F32)[:, None] * freqs[None, :]
    ang_c = cols.astype(F32)[:, None] * freqs[None, :]
    cr, sr, cc, sc = jnp.cos(ang_r), jnp.sin(ang_r), jnp.cos(ang_c), jnp.sin(ang_c)
    c64 = jnp.concatenate([cr, cr, cc, cc], axis=-1)
    s64 = jnp.concatenate([-sr, sr, -sc, sc], axis=-1)
    c_lat = jnp.tile(c64, (DEC_BATCH, LANES // WIN_HEAD_DIM))
    s_lat = jnp.tile(s64, (DEC_BATCH, LANES // WIN_HEAD_DIM))
    cos_t = jnp.concatenate([jnp.ones((N_CTX, LANES), F32), c_lat], axis=0)
    sin_t = jnp.concatenate([jnp.zeros((N_CTX, LANES), F32), s_lat], axis=0)
    return cos_t, sin_t


def kernel(x_prompt, x_sample, cache_win_k, cache_win_v, cache_mla_ckv, cache_mla_kpe, c, c_ctx, ada_w, ada_b, norm_mix, norm_ffn, win_w_qkv, win_w_o, win_sink, mla_w_in, mla_q_norm, mla_w_q_b, mla_kv_norm, mla_w_kv_b, mla_w_o, ffn_w_gate, ffn_w_up, ffn_w_down, norm_final):
    x = [x_prompt.reshape(N_CTX, D_MODEL), x_sample.reshape(N_LAT, D_MODEL)]
    cond = jnp.concatenate([c_ctx[None, :], c, jnp.zeros((MOD_ROWS - 1 - DEC_BATCH, D_MODEL), F32)], axis=0)
    mods = [_ada_call(cond, ada_w, ada_b, 0).reshape(MOD_ROWS, 1, 6 * D_MODEL)]
    tables = _rope_tables()
    cache_k = cache_win_k.reshape(DEC_BATCH, N_WIN_LAYERS, PAST_LEN, WIN_KD)
    cache_v = cache_win_v.reshape(DEC_BATCH, N_WIN_LAYERS, PAST_LEN, WIN_KD)

    kv_layers, ckv_layers, kpe_layers = [], [], []
    mix_gains = norm_mix.reshape(DEPTH, 1, D_MODEL)
    ffn_gains = norm_ffn.reshape(DEPTH, 1, D_MODEL)
    win_row_perm = [8 * p + 4 * s + t for p in range(WIN_KV_HEADS // 2) for t in range(WIN_GROUP) for s in range(2)]
    h = _modulate_call(x[0], x[1], norm_mix, 0, mods[0], 0, 1)
    for layer in range(DEPTH):
        j = layer // 2
        if layer % 2 == 0:
            q = _proj_call(h, win_w_qkv, j, 0, WIN_QD, BF16, tm=1024, tn=1024,
                           rope=(0, WIN_QD // 1024), tables=tables, pair_order=True, name="win_q_proj")
            kv = _proj_call(h, win_w_qkv, j, WIN_QD, 2 * WIN_KD, F32, tm=1024, tn=512,
                            rope=(0, 1), tables=tables, name="win_kv_proj")
            o_ctx = _win_ctx_call(q, kv, win_sink, j)
            o_lat = _win_lat_call(q, kv, cache_k, cache_v, win_sink, j)
            w_o, row_perm = win_w_o, win_row_perm
            kv_layers.append(kv)
        else:
            cq, ckv, kpe = _mla_in_call(h, mla_w_in, j, mla_q_norm[j], mla_kv_norm[j], tables)
            wq = mla_w_q_b[j].reshape(Q_LORA_RANK, MLA_HEADS, QK_NOPE_DIM + QK_ROPE_DIM)
            wq = jnp.concatenate([wq[:, :, :QK_NOPE_DIM].reshape(Q_LORA_RANK, MLA_NOPE_W),
                                  wq[:, :, QK_NOPE_DIM:].reshape(Q_LORA_RANK, MLA_ROPE_W)], axis=1)
            q, kvx = _mla_expand_call(cq, ckv, wq, mla_w_kv_b, j, tables)
            cache_kpe = jnp.tile(cache_mla_kpe[:, j], (1, 1, LANES // QK_ROPE_DIM))
            o_ctx = _mla_ctx_call(q, kvx, kpe)
            o_lat = _mla_lat_call(q, kvx, cache_mla_ckv, mla_w_kv_b, j, kpe, cache_kpe)
            w_o, row_perm = mla_w_o, None
            ckv_layers.append(ckv)
            kpe_layers.append(kpe)
        xn, h = _resid_norm_call([o_ctx, o_lat], w_o, j, x, mods[layer], 2, ffn_gains, layer,
                                 (mods[layer], 3, 4), tm=256, row_perm=row_perm, name="mixer_out_proj")
        if layer + 1 < DEPTH:
            act, w_down, mod_next = _swiglu_call(h, ffn_w_gate, ffn_w_up, ffn_w_down, layer,
                                                 ada=(cond, ada_w, ada_b))
            mods.append(mod_next.reshape(MOD_ROWS, 1, 6 * D_MODEL))
            xn, h = _resid_norm_call([act], w_down, 0, [xn], mods[layer], 5, mix_gains, layer + 1,
                                     (mods[layer + 1], 0, 1), tm=256, name="ffn_down_proj")
            x = [xn]
        else:
            act, w_down = _swiglu_call(h, ffn_w_gate, ffn_w_up, ffn_w_down, layer)
            y_ctx, y_lat = _resid_norm_call([act], w_down, 0, [xn], mods[layer], 5,
                                            norm_final.reshape(1, 1, D_MODEL), 0, None,
                                            tm=256, name="ffn_down_final")

    y_prompt = y_ctx.reshape(BATCH, SEQ, D_MODEL)
    y_sample = y_lat.reshape(DEC_BATCH, DEC_SEQ, D_MODEL)
    new_k, new_v = _pack_call([kv_layers], [(0, 0, WIN_KD), (0, WIN_KD, 2 * WIN_KD)], "pack_win_cache")
    new_ckv, new_kpe = _pack_call([ckv_layers, kpe_layers], [(0, 0, KV_LORA_RANK), (1, 0, QK_ROPE_DIM)],
                                  "pack_mla_cache")
    head_shape = (BATCH, N_WIN_LAYERS, SEQ, WIN_KV_HEADS, WIN_HEAD_DIM)
    return (y_prompt, y_sample, new_k.reshape(head_shape), new_v.reshape(head_shape), new_ckv, new_kpe)
```

```python
import functools
import math

import numpy as np
import jax
import jax.numpy as jnp
from jax import lax
from jax.experimental import pallas as pl
from jax.experimental.pallas import tpu as pltpu

F32 = jnp.float32
BF16 = jnp.bfloat16

D_MODEL = 2048
BATCH = 16
SEQ = 256
DEPTH = 4
DEC_BATCH = 2
DEC_SEQ = 2048
PAST_LEN = 256
GRID_W = 64
N_WIN_LAYERS = 2
N_MLA_LAYERS = 2
WIN_HEADS = 32
WIN_KV_HEADS = 8
WIN_GROUP = WIN_HEADS // WIN_KV_HEADS
WIN_HEAD_DIM = 64
WINDOW = 128
BLOCK = 128
WIN_SCALE = WIN_HEAD_DIM ** -0.5
MLA_HEADS = 16
Q_LORA_RANK = 512
KV_LORA_RANK = 512
QK_NOPE_DIM = 128
QK_ROPE_DIM = 64
V_HEAD_DIM = 128
MLA_SCALE = (QK_NOPE_DIM + QK_ROPE_DIM) ** -0.5
D_FF = 5632
ROPE_BASE = 10000.0
EPS = 1e-6
NEG = float(np.finfo(np.float32).min)
LOG2E = math.log2(math.e)

N_CTX = BATCH * SEQ
N_LAT = DEC_BATCH * DEC_SEQ
M_ALL = N_CTX + N_LAT
MOD_ROWS = 16
LANES = 128
HALF_LANES = LANES // 2
WIN_QD = WIN_HEADS * WIN_HEAD_DIM
WIN_KD = WIN_KV_HEADS * WIN_HEAD_DIM
VMEM_LIMIT = 56 * 1024 * 1024
VMEM_LIMIT_LARGE = 60 * 1024 * 1024


def _params(n_axes, vmem=VMEM_LIMIT):
    return pltpu.CompilerParams(dimension_semantics=("arbitrary",) * n_axes,
                                vmem_limit_bytes=vmem)


def _group(i, tm):
    start = i * tm
    return jnp.where(start < N_CTX, 0, 1 + (start - N_CTX) // DEC_SEQ)


def _mod_spec(chunk, tm, tn, m_axis, n_axis):
    nb = D_MODEL // tn

    def imap(*ids):
        j = 0 if n_axis is None else ids[n_axis]
        return (_group(ids[m_axis], tm), 0, chunk * nb + j)

    return pl.BlockSpec((None, 1, tn), imap)


def _rms(x, g):
    y = x * lax.rsqrt(jnp.mean(x * x, axis=-1, keepdims=True) + EPS)
    return y * g


def _dot(a, b):
    return jnp.dot(a, b, preferred_element_type=F32)


def _dot_nt(a, b):
    return lax.dot_general(a, b, (((1,), (1,)), ((), ())), preferred_element_type=F32)


def _rope(x, c, s):
    lane = lax.broadcasted_iota(jnp.int32, (x.shape[0], LANES), 1)
    first = (lane & 31) < 16
    outs = []
    for k in range(x.shape[1] // LANES):
        a = x[:, k * LANES:(k + 1) * LANES]
        partner = jnp.where(first, pltpu.roll(a, LANES - 16, 1), pltpu.roll(a, 16, 1))
        outs.append(a * c + partner * s)
    return outs[0] if len(outs) == 1 else jnp.concatenate(outs, axis=1)


def _ada_body(cond_ref, w_ref, b_ref, o_ref):
    s = jax.nn.silu(cond_ref[...])
    o_ref[...] = _dot(s.astype(BF16), w_ref[...].astype(BF16)) + b_ref[...]


def _ada_call(cond, ada_w, ada_b, layer):
    tn = 1024
    n = 6 * D_MODEL
    return pl.pallas_call(
        _ada_body,
        grid=(n // tn,),
        in_specs=[
            pl.BlockSpec((MOD_ROWS, D_MODEL), lambda j: (0, 0)),
            pl.BlockSpec((None, D_MODEL, tn), lambda j: (layer, 0, j)),
            pl.BlockSpec((None, 1, tn), lambda j: (layer, 0, j)),
        ],
        out_specs=pl.BlockSpec((MOD_ROWS, tn), lambda j: (0, j)),
        out_shape=jax.ShapeDtypeStruct((MOD_ROWS, n), F32),
        compiler_params=_params(1),
        name="ada_modulation",
    )(cond, ada_w, ada_b.reshape(DEPTH, 1, n))


def _part_specs(n, tm, width):
    ctx_tiles = N_CTX // tm
    if n == 1:
        return [pl.BlockSpec((tm, width), lambda i: (i, 0))]
    return [pl.BlockSpec((tm, width), lambda i: (jnp.minimum(i, ctx_tiles - 1), 0)),
            pl.BlockSpec((tm, width), lambda i: (jnp.maximum(i - ctx_tiles, 0), 0))]


def _modulate_body(xc_ref, xl_ref, g_ref, sc_ref, sh_ref, o_ref, *, ctx_tiles):
    def run(x_ref):
        y = _rms(x_ref[...], g_ref[...])
        o_ref[...] = (y * (1.0 + sc_ref[...]) + sh_ref[...]).astype(o_ref.dtype)

    i = pl.program_id(0)
    pl.when(i < ctx_tiles)(lambda: run(xc_ref))
    pl.when(i >= ctx_tiles)(lambda: run(xl_ref))


def _modulate_call(x_ctx, x_lat, gains, layer, mod, shift_chunk, scale_chunk):
    tm = 512
    return pl.pallas_call(
        functools.partial(_modulate_body, ctx_tiles=N_CTX // tm),
        grid=(M_ALL // tm,),
        in_specs=_part_specs(2, tm, D_MODEL) + [
            pl.BlockSpec((None, 1, D_MODEL), lambda i: (layer, 0, 0)),
            _mod_spec(scale_chunk, tm, D_MODEL, 0, None),
            _mod_spec(shift_chunk, tm, D_MODEL, 0, None),
        ],
        out_specs=pl.BlockSpec((tm, D_MODEL), lambda i: (i, 0)),
        out_shape=jax.ShapeDtypeStruct((M_ALL, D_MODEL), BF16),
        compiler_params=_params(1),
        name="norm_modulate",
    )(x_ctx, x_lat, gains.reshape(DEPTH, 1, D_MODEL), mod, mod)


def _low_lanes(rows):
    return lax.broadcasted_iota(jnp.int32, (rows, LANES), 1) < HALF_LANES


def _cast_pair_order(w_ref, w_s):
    low = _low_lanes(w_ref.shape[0])
    per_pair = 2 * WIN_GROUP * WIN_HEAD_DIM // LANES
    for p in range(w_ref.shape[1] // (per_pair * LANES)):
        c = [w_ref[:, (per_pair * p + t) * LANES:(per_pair * p + t + 1) * LANES] for t in range(per_pair)]
        r = [pltpu.roll(x, HALF_LANES, 1) for x in c]
        moved = [jnp.where(low, c[0], r[2]), jnp.where(low, r[0], c[2]),
                 jnp.where(low, c[1], r[3]), jnp.where(low, r[1], c[3])]
        for t, x in enumerate(moved):
            w_s[:, (per_pair * p + t) * LANES:(per_pair * p + t + 1) * LANES] = x.astype(BF16)


PROJ_SPLIT = 4


def _proj_body(*refs, rope_lo, rope_hi, rope_cols, n_ctx_tiles, pair_order):
    has_rope = rope_hi > rope_lo
    if has_rope:
        lhs_ref, w_ref, cos_ref, sin_ref, o_ref, w_s = refs
    else:
        lhs_ref, w_ref, o_ref, w_s = refs
    j = pl.program_id(0)
    i = pl.program_id(1)

    @pl.when(i == 0)
    def _():
        if pair_order:
            _cast_pair_order(w_ref, w_s)
        else:
            w_s[...] = w_ref[...].astype(BF16)

    def run(with_rope):
        sub = lhs_ref.shape[0] // PROJ_SPLIT
        pending = None
        for s in range(PROJ_SPLIT + 1):
            if s < PROJ_SPLIT:
                rows = slice(s * sub, (s + 1) * sub)
                cur = (rows, _dot(lhs_ref[rows, :].astype(BF16), w_s[...]))
            if pending is not None:
                prow, acc = pending
                if with_rope:
                    roped_part = _rope(acc[:, :rope_cols], cos_ref[prow, :], sin_ref[prow, :])
                    rest = [acc[:, rope_cols:]] if rope_cols < acc.shape[1] else []
                    acc = jnp.concatenate([roped_part] + rest, axis=1) if rest else roped_part
                o_ref[prow, :] = acc.astype(o_ref.dtype)
            pending = cur if s < PROJ_SPLIT else None

    if not has_rope:
        run(False)
    else:
        roped = jnp.logical_and(jnp.logical_and(j >= rope_lo, j < rope_hi), i >= n_ctx_tiles)
        pl.when(roped)(lambda: run(True))
        pl.when(jnp.logical_not(roped))(lambda: run(False))


def _proj_call(lhs, w, layer, col_off, n_cols, out_dtype, *, tm, tn, rope=None, rope_cols=None, tables=None,
               pair_order=False, name):
    m, k = lhs.shape
    n_tiles = n_cols // tn
    off = col_off // tn
    rope_lo, rope_hi = rope if rope is not None else (0, 0)
    if w.ndim == 3:
        w_spec = pl.BlockSpec((None, k, tn), lambda j, i: (layer, 0, off + j))
    else:
        w_spec = pl.BlockSpec((k, tn), lambda j, i: (0, off + j))
    in_specs = [pl.BlockSpec((tm, k), lambda j, i: (i, 0)), w_spec]
    args = [lhs, w]
    if rope_hi > rope_lo:
        in_specs += [pl.BlockSpec((tm, LANES), lambda j, i: (i, 0))] * 2
        args += list(tables)
    return pl.pallas_call(
        functools.partial(_proj_body, rope_lo=rope_lo, rope_hi=rope_hi, rope_cols=rope_cols or tn,
                          n_ctx_tiles=N_CTX // tm, pair_order=pair_order),
        grid=(n_tiles, m // tm),
        in_specs=in_specs,
        out_specs=pl.BlockSpec((tm, tn), lambda j, i: (i, j)),
        out_shape=jax.ShapeDtypeStruct((m, n_cols), out_dtype),
        scratch_shapes=[pltpu.VMEM((k, tn), BF16)],
        compiler_params=_params(2),
        name=name,
    )(*args)


ROW_BLOCK = WIN_HEAD_DIM


def _resid_norm_body(*refs, n_parts, n_x, ctx_tiles, cast_w, row_perm, final):
    lhs_refs = refs[:n_parts]
    w_ref = refs[n_parts]
    x_refs = refs[n_parts + 1:n_parts + 1 + n_x]
    n_mod = 2 if final else 4
    mods = refs[n_parts + 1 + n_x:n_parts + 1 + n_x + n_mod]
    outs = refs[n_parts + 1 + n_x + n_mod:n_parts + 1 + n_x + n_mod + 2]
    scratch = refs[n_parts + 1 + n_x + n_mod + 2:]
    gate_ref, g_ref = mods[:2]
    i = pl.program_id(0)
    if cast_w:
        w = scratch[0]

        @pl.when(i == 0)
        def _():
            if row_perm is None:
                w[...] = w_ref[...].astype(BF16)
            else:
                for new, old in enumerate(row_perm):
                    w[new * ROW_BLOCK:(new + 1) * ROW_BLOCK, :] = (
                        w_ref[old * ROW_BLOCK:(old + 1) * ROW_BLOCK, :].astype(BF16))
    else:
        w = w_ref

    def step(lhs_ref, x_ref, y_ref):
        xn = x_ref[...] + gate_ref[...] * _dot(lhs_ref[...], w[...])
        y = _rms(xn, g_ref[...])
        if final:
            y_ref[...] = y
        else:
            outs[0][...] = xn
            outs[1][...] = (y * (1.0 + mods[2][...]) + mods[3][...]).astype(outs[1].dtype)

    if n_parts == 1 and n_x == 1 and not final:
        step(lhs_refs[0], x_refs[0], None)
    else:
        pl.when(i < ctx_tiles)(lambda: step(lhs_refs[0], x_refs[0], outs[0]))
        pl.when(i >= ctx_tiles)(lambda: step(lhs_refs[-1], x_refs[-1], outs[1]))


def _resid_norm_call(lhs_parts, w, w_layer, x_parts, mod, gate_chunk, gains, gain_layer, next_mod, *,
                     tm, row_perm=None, name):
    k = lhs_parts[0].shape[1]
    n_parts = len(lhs_parts)
    n_x = len(x_parts)
    ctx_tiles = N_CTX // tm
    final = next_mod is None
    cast_w = w.dtype != BF16
    if w.ndim == 3:
        w_spec = pl.BlockSpec((None, k, D_MODEL), lambda i: (w_layer, 0, 0), pipeline_mode=pl.Buffered(1))
    else:
        w_spec = pl.BlockSpec((k, D_MODEL), lambda i: (0, 0))
    in_specs = _part_specs(n_parts, tm, k) + [w_spec] + _part_specs(n_x, tm, D_MODEL) + [
        _mod_spec(gate_chunk, tm, D_MODEL, 0, None),
        pl.BlockSpec((None, 1, D_MODEL), lambda i: (gain_layer, 0, 0)),
    ]
    args = list(lhs_parts) + [w] + list(x_parts) + [mod, gains]
    if final:
        out_specs = _part_specs(2, tm, D_MODEL)
        out_shape = [jax.ShapeDtypeStruct((N_CTX, D_MODEL), F32), jax.ShapeDtypeStruct((N_LAT, D_MODEL), F32)]
    else:
        mod_next, shift_chunk, scale_chunk = next_mod
        in_specs += [_mod_spec(scale_chunk, tm, D_MODEL, 0, None),
                     _mod_spec(shift_chunk, tm, D_MODEL, 0, None)]
        args += [mod_next, mod_next]
        out_specs = _part_specs(1, tm, D_MODEL) * 2
        out_shape = [jax.ShapeDtypeStruct((M_ALL, D_MODEL), F32), jax.ShapeDtypeStruct((M_ALL, D_MODEL), BF16)]
    return pl.pallas_call(
        functools.partial(_resid_norm_body, n_parts=n_parts, n_x=n_x, ctx_tiles=ctx_tiles, cast_w=cast_w,
                          row_perm=row_perm, final=final),
        grid=(M_ALL // tm,),
        in_specs=in_specs,
        out_specs=out_specs,
        out_shape=out_shape,
        scratch_shapes=[pltpu.VMEM((k, D_MODEL), BF16)] if cast_w else [],
        compiler_params=_params(1),
        name=name,
    )(*args)


SWIGLU_SPLIT = 8


ADA_CHUNK = 3 * LANES
ADA_STEPS = 6 * D_MODEL // ADA_CHUNK


def _swiglu_body(*refs, with_ada):
    if with_ada:
        h_ref, wg_ref, wu_ref, wd_ref, ct_ref, aw_ref, ab_ref, o_ref, wdo_ref, mod_ref, wg_s, wu_s, sb_s = refs
    else:
        h_ref, wg_ref, wu_ref, wd_ref, o_ref, wdo_ref, wg_s, wu_s = refs

    @pl.when(pl.program_id(1) == 0)
    def _():
        wg_s[...] = wg_ref[...].astype(BF16)
        wu_s[...] = wu_ref[...].astype(BF16)

    if with_ada:
        @pl.when(jnp.logical_and(pl.program_id(0) == 0, pl.program_id(1) == 0))
        def _():
            sb_s[...] = jax.nn.silu(ct_ref[...]).astype(BF16)

        def ada_piece():
            mod_ref[...] = _dot(sb_s[...], aw_ref[...].astype(BF16)) + ab_ref[...]

        side = [ada_piece]
    else:
        side = []

    sub = h_ref.shape[0] // SWIGLU_SPLIT
    pending = None
    for s in range(SWIGLU_SPLIT + 1):
        if s < SWIGLU_SPLIT:
            rows = slice(s * sub, (s + 1) * sub)
            h = h_ref[rows, :]
            cur = (rows, _dot(h, wg_s[...]), _dot(h, wu_s[...]))
        if pending is not None:
            prow, g, u = pending
            o_ref[prow, :] = (jax.nn.silu(g) * u).astype(o_ref.dtype)
        pending = cur if s < SWIGLU_SPLIT else None
        for piece in side[s::SWIGLU_SPLIT + 1]:
            piece()
    wdo_ref[...] = wd_ref[...].astype(BF16)


def _swiglu_call(h, w_gate, w_up, w_down, layer, ada=None):
    tm, tn = 2048, 512
    n_i = M_ALL // tm
    slab = D_FF // ((D_FF // tn) * n_i)
    step = lambda j, i: j * n_i + i
    in_specs = [
        pl.BlockSpec((tm, D_MODEL), lambda j, i: (i, 0)),
        pl.BlockSpec((None, D_MODEL, tn), lambda j, i: (layer, 0, j)),
        pl.BlockSpec((None, D_MODEL, tn), lambda j, i: (layer, 0, j)),
        pl.BlockSpec((None, slab, D_MODEL), lambda j, i: (layer, step(j, i), 0)),
    ]
    out_specs = [
        pl.BlockSpec((tm, tn), lambda j, i: (i, j)),
        pl.BlockSpec((slab, D_MODEL), lambda j, i: (step(j, i), 0)),
    ]
    out_shape = [jax.ShapeDtypeStruct((M_ALL, D_FF), BF16), jax.ShapeDtypeStruct((D_FF, D_MODEL), BF16)]
    scratch = [pltpu.VMEM((D_MODEL, tn), BF16), pltpu.VMEM((D_MODEL, tn), BF16)]
    args = [h, w_gate, w_up, w_down]
    if ada is not None:
        assert (D_FF // tn) * n_i >= ADA_STEPS
        cond, ada_w, ada_b = ada
        chunk = lambda j, i: jnp.minimum(step(j, i), ADA_STEPS - 1)
        in_specs += [
            pl.BlockSpec((MOD_ROWS, D_MODEL), lambda j, i: (0, 0)),
            pl.BlockSpec((None, D_MODEL, ADA_CHUNK), lambda j, i: (layer + 1, 0, chunk(j, i))),
            pl.BlockSpec((None, 1, ADA_CHUNK), lambda j, i: (layer + 1, 0, chunk(j, i))),
        ]
        out_specs.append(pl.BlockSpec((MOD_ROWS, ADA_CHUNK), lambda j, i: (0, chunk(j, i))))
        out_shape.append(jax.ShapeDtypeStruct((MOD_ROWS, 6 * D_MODEL), F32))
        scratch.append(pltpu.VMEM((MOD_ROWS, D_MODEL), BF16))
        args += [cond, ada_w, ada_b.reshape(DEPTH, 1, 6 * D_MODEL)]
    return pl.pallas_call(
        functools.partial(_swiglu_body, with_ada=ada is not None),
        grid=(D_FF // tn, n_i),
        in_specs=in_specs,
        out_specs=out_specs,
        out_shape=out_shape,
        scratch_shapes=scratch,
        compiler_params=_params(2, vmem=VMEM_LIMIT_LARGE),
        name="ffn_gate_up",
    )(*args)


def _mla_in_body(h_ref, w_ref, qn_ref, kvn_ref, cos_ref, sin_ref, cq_ref, ckv_ref, kpe_ref, w_s):
    @pl.when(pl.program_id(0) == 0)
    def _():
        w_s[...] = w_ref[...].astype(BF16)

    def epilogue(rows, acc):
        cq = acc[:, :Q_LORA_RANK]
        ckv = acc[:, Q_LORA_RANK:Q_LORA_RANK + KV_LORA_RANK]
        kpe = acc[:, Q_LORA_RANK + KV_LORA_RANK:]
        kpe = jnp.concatenate([kpe, kpe], axis=1)
        cq_ref[rows, :] = _rms(cq, qn_ref[...]).astype(cq_ref.dtype)
        ckv_ref[rows, :] = _rms(ckv, kvn_ref[...])
        kpe_ref[rows, :] = _rope(kpe, cos_ref[rows, :], sin_ref[rows, :])

    sub = h_ref.shape[0] // PROJ_SPLIT
    pending = None
    for s in range(PROJ_SPLIT + 1):
        if s < PROJ_SPLIT:
            rows = slice(s * sub, (s + 1) * sub)
            cur = (rows, _dot(h_ref[rows, :], w_s[...]))
        if pending is not None:
            epilogue(*pending)
        pending = cur if s < PROJ_SPLIT else None


def _mla_in_call(h, w_in, layer, q_norm, kv_norm, tables):
    tm = 1024
    row = lambda i: (i, 0)
    fixed = lambda i: (0, 0)
    n_in = Q_LORA_RANK + KV_LORA_RANK + QK_ROPE_DIM
    return pl.pallas_call(
        _mla_in_body,
        grid=(M_ALL // tm,),
        in_specs=[
            pl.BlockSpec((tm, D_MODEL), row),
            pl.BlockSpec((None, D_MODEL, n_in), lambda i: (layer, 0, 0), pipeline_mode=pl.Buffered(1)),
            pl.BlockSpec((1, Q_LORA_RANK), fixed),
            pl.BlockSpec((1, KV_LORA_RANK), fixed),
            pl.BlockSpec((tm, LANES), row),
            pl.BlockSpec((tm, LANES), row),
        ],
        out_specs=[
            pl.BlockSpec((tm, Q_LORA_RANK), row),
            pl.BlockSpec((tm, KV_LORA_RANK), row),
            pl.BlockSpec((tm, LANES), row),
        ],
        out_shape=[
            jax.ShapeDtypeStruct((M_ALL, Q_LORA_RANK), BF16),
            jax.ShapeDtypeStruct((M_ALL, KV_LORA_RANK), F32),
            jax.ShapeDtypeStruct((M_ALL, LANES), F32),
        ],
        scratch_shapes=[pltpu.VMEM((D_MODEL, n_in), BF16)],
        compiler_params=_params(1),
        name="mla_in_proj",
    )(h, w_in, q_norm.reshape(1, -1), kv_norm.reshape(1, -1), *tables)


def _mla_expand_body(cq_ref, ckv_ref, wq_ref, wkv_ref, cos_ref, sin_ref, q_ref, kvx_ref, wq_s, wkv_s, *,
                     n_ctx_tiles, nope_w):
    i = pl.program_id(0)

    @pl.when(i == 0)
    def _():
        wq_s[...] = wq_ref[...].astype(BF16)
        wkv_s[...] = wkv_ref[...].astype(BF16)

    def run(with_rope):
        sub = cq_ref.shape[0] // PROJ_SPLIT
        pending = None
        for s in range(PROJ_SPLIT + 1):
            if s < PROJ_SPLIT:
                rows = slice(s * sub, (s + 1) * sub)
                cur = (rows, _dot(cq_ref[rows, :], wq_s[...]),
                       _dot(ckv_ref[rows, :].astype(BF16), wkv_s[...]))
            if pending is not None:
                prow, q, kvx = pending
                q_ref[prow, :nope_w] = q[:, :nope_w].astype(q_ref.dtype)
                qp = q[:, nope_w:]
                if with_rope:
                    qp = _rope(qp, cos_ref[prow, :], sin_ref[prow, :])
                q_ref[prow, nope_w:] = qp.astype(q_ref.dtype)
                kvx_ref[prow, :] = kvx.astype(kvx_ref.dtype)
            pending = cur if s < PROJ_SPLIT else None

    pl.when(i >= n_ctx_tiles)(lambda: run(True))
    pl.when(i < n_ctx_tiles)(lambda: run(False))


def _mla_expand_call(cq, ckv, wq, w_kv_b, layer, tables):
    tm = 512
    n_q = wq.shape[1]
    n_kv = w_kv_b.shape[2]
    row = lambda i: (i, 0)
    return pl.pallas_call(
        functools.partial(_mla_expand_body, n_ctx_tiles=N_CTX // tm, nope_w=MLA_HEADS * QK_NOPE_DIM),
        grid=(M_ALL // tm,),
        in_specs=[
            pl.BlockSpec((tm, Q_LORA_RANK), row),
            pl.BlockSpec((tm, KV_LORA_RANK), row),
            pl.BlockSpec((Q_LORA_RANK, n_q), lambda i: (0, 0)),
            pl.BlockSpec((None, KV_LORA_RANK, n_kv), lambda i: (layer, 0, 0), pipeline_mode=pl.Buffered(1)),
            pl.BlockSpec((tm, LANES), row),
            pl.BlockSpec((tm, LANES), row),
        ],
        out_specs=[pl.BlockSpec((tm, n_q), row), pl.BlockSpec((tm, n_kv), row)],
        out_shape=[jax.ShapeDtypeStruct((M_ALL, n_q), BF16), jax.ShapeDtypeStruct((M_ALL, n_kv), BF16)],
        scratch_shapes=[pltpu.VMEM((Q_LORA_RANK, n_q), BF16), pltpu.VMEM((KV_LORA_RANK, n_kv), BF16)],
        compiler_params=_params(1),
        name="mla_expand",
    )(cq, ckv, wq, w_kv_b, *tables)


def _win_heads(q_ref, o_ref, sink_ref, layer, k_chunks, v_chunks, blocks, nq, pipelined):
    low_q = _low_lanes(nq)
    qlo = jnp.where(low_q, WIN_SCALE, 0.0).astype(BF16)
    qhi = jnp.where(low_q, 0.0, WIN_SCALE).astype(BF16)
    heads_per_pair = 2 * WIN_GROUP

    def scores(p):
        kc = k_chunks[p].astype(BF16)
        qcs = [q_ref[:, (WIN_GROUP * p + t) * LANES:(WIN_GROUP * p + t + 1) * LANES] for t in range(WIN_GROUP)]
        qs = jnp.concatenate([qc * qlo for qc in qcs] + [qc * qhi for qc in qcs], axis=0)
        return _dot_nt(qs, kc)

    def softmax(p, lg):
        probs, invs = [], []
        for u in range(heads_per_pair):
            sk = sink_ref[layer, heads_per_pair * p + u]
            l = lg[u * nq:(u + 1) * nq]
            cols = [l[:, a:a + LANES] if msk is None else jnp.where(msk, l[:, a:a + LANES], NEG)
                    for a, msk in blocks]
            mx = cols[0]
            for col in cols[1:]:
                mx = jnp.maximum(mx, col)
            m = jnp.maximum(jnp.max(mx, axis=-1, keepdims=True), sk)
            es = [jnp.exp(col - m) for col in cols]
            tot = es[0]
            for e in es[1:]:
                tot = tot + e
            invs.append(1.0 / (jnp.sum(tot, axis=-1, keepdims=True) + jnp.exp(sk - m)))
            probs.append(jnp.concatenate([e.astype(BF16) for e in es], axis=1))
        return jnp.concatenate(probs, axis=0), invs

    def values(p, probs, invs):
        o = _dot(probs, v_chunks[p].astype(BF16))
        on = [o[u * nq:(u + 1) * nq] * invs[u] for u in range(heads_per_pair)]
        for t in range(WIN_GROUP):
            c = WIN_GROUP * p + t
            o_ref[:, c * LANES:(c + 1) * LANES] = jnp.where(low_q, on[t], on[WIN_GROUP + t]).astype(o_ref.dtype)

    n = WIN_KV_HEADS // 2
    if not pipelined:
        for p in range(n):
            values(p, *softmax(p, scores(p)))
        return
    sc, pr = {}, {}
    for p in range(n + 2):
        if p < n:
            sc[p] = scores(p)
        if 0 <= p - 1 < n:
            pr[p - 1] = softmax(p - 1, sc.pop(p - 1))
        if 0 <= p - 2 < n:
            values(p - 2, *pr.pop(p - 2))


def _win_ctx_body(sink_ref, q_ref, kv_ref, o_ref, *, layer):
    n_pairs = WIN_KD // LANES
    k_chunks = [kv_ref[:, p * LANES:(p + 1) * LANES] for p in range(n_pairs)]
    v_chunks = [kv_ref[:, WIN_KD + p * LANES:WIN_KD + (p + 1) * LANES] for p in range(n_pairs)]
    _win_heads(q_ref, o_ref, sink_ref, layer, k_chunks, v_chunks,
               [(a, None) for a in range(0, SEQ, LANES)], SEQ, False)


def _win_ctx_call(q, kv, sink, layer):
    return pl.pallas_call(
        functools.partial(_win_ctx_body, layer=layer),
        grid=(BATCH,),
        in_specs=[
            pl.BlockSpec(memory_space=pltpu.SMEM),
            pl.BlockSpec((SEQ, WIN_QD), lambda b: (b, 0)),
            pl.BlockSpec((SEQ, 2 * WIN_KD), lambda b: (b, 0)),
        ],
        out_specs=pl.BlockSpec((SEQ, WIN_QD), lambda b: (b, 0)),
        out_shape=jax.ShapeDtypeStruct((N_CTX, WIN_QD), BF16),
        compiler_params=_params(1),
        name="win_attn_context",
    )(sink, q, kv)


def _win_lat_body(sink_ref, q_ref, kvp_ref, kvc_ref, kvn_ref, ck_ref, cv_ref, o_ref, *, layer):
    n = pl.program_id(1)
    n_pairs = WIN_KD // LANES
    k_chunks, v_chunks = [], []
    for p in range(n_pairs):
        ks = slice(p * LANES, (p + 1) * LANES)
        vs = slice(WIN_KD + p * LANES, WIN_KD + (p + 1) * LANES)
        k_chunks.append(jnp.concatenate([kvp_ref[:, ks], kvc_ref[:, ks], kvn_ref[:, ks], ck_ref[:, ks]], axis=0))
        v_chunks.append(jnp.concatenate([kvp_ref[:, vs], kvc_ref[:, vs], kvn_ref[:, vs], cv_ref[:, ks]], axis=0))
    r = lax.broadcasted_iota(jnp.int32, (BLOCK, BLOCK), 0)
    s = lax.broadcasted_iota(jnp.int32, (BLOCK, BLOCK), 1)
    prev_ok = (s - r) >= jnp.where(n >= 1, 0, BLOCK)
    next_ok = (r - s) >= jnp.where(n <= DEC_SEQ // BLOCK - 2, 0, BLOCK)
    blocks = [(0, prev_ok), (BLOCK, None), (2 * BLOCK, next_ok)]
    blocks += [(3 * BLOCK + a, None) for a in range(0, PAST_LEN, LANES)]
    _win_heads(q_ref, o_ref, sink_ref, layer, k_chunks, v_chunks, blocks, BLOCK, True)


def _win_lat_call(q, kv, cache_k, cache_v, sink, layer):
    nb = DEC_SEQ // BLOCK
    base = N_CTX // BLOCK

    def rows(shift):
        return lambda b, n: (base + b * nb + jnp.clip(n + shift, 0, nb - 1), 0)

    cache_spec = pl.BlockSpec((None, None, PAST_LEN, WIN_KD), lambda b, n: (b, layer, 0, 0))
    return pl.pallas_call(
        functools.partial(_win_lat_body, layer=layer),
        grid=(DEC_BATCH, nb),
        in_specs=[
            pl.BlockSpec(memory_space=pltpu.SMEM),
            pl.BlockSpec((BLOCK, WIN_QD), rows(0)),
            pl.BlockSpec((BLOCK, 2 * WIN_KD), rows(-1)),
            pl.BlockSpec((BLOCK, 2 * WIN_KD), rows(0)),
            pl.BlockSpec((BLOCK, 2 * WIN_KD), rows(1)),
            cache_spec,
            cache_spec,
        ],
        out_specs=pl.BlockSpec((BLOCK, WIN_QD), lambda b, n: (b * nb + n, 0)),
        out_shape=jax.ShapeDtypeStruct((N_LAT, WIN_QD), BF16),
        compiler_params=_params(2),
        name="win_attn_latent",
    )(sink, q, kv, kv, kv, cache_k, cache_v)


KV_HEAD_W = QK_NOPE_DIM + V_HEAD_DIM
MLA_NOPE_W = MLA_HEADS * QK_NOPE_DIM
MLA_ROPE_W = MLA_HEADS * QK_ROPE_DIM
MLA_EXP2_SCALE = MLA_SCALE * LOG2E


def _half_masks_bf16(rows):
    low = _low_lanes(rows)
    return jnp.where(low, 1.0, 0.0).astype(BF16), jnp.where(low, 0.0, 1.0).astype(BF16)


def _mla_ctx_body(qn_ref, qp_ref, kvx_ref, kpe_ref, o_ref):
    qlo, qhi = _half_masks_bf16(SEQ)
    kpe2 = kpe_ref[...].astype(BF16)
    def scores(h):
        qp = qp_ref[:, (h // 2) * LANES:(h // 2 + 1) * LANES] * (qlo if h % 2 == 0 else qhi)
        qcat = jnp.concatenate([qn_ref[:, h * LANES:(h + 1) * LANES], qp], axis=1)
        kcat = jnp.concatenate([kvx_ref[:, h * KV_HEAD_W:h * KV_HEAD_W + QK_NOPE_DIM], kpe2], axis=1)
        return _dot_nt(qcat, kcat)

    def softmax(lg):
        m = jnp.max(lg, axis=-1, keepdims=True)
        e = jnp.exp2((lg - m) * MLA_EXP2_SCALE)
        return e.astype(BF16), 1.0 / jnp.sum(e, axis=-1, keepdims=True)

    def values(h, p, inv):
        o = _dot(p, kvx_ref[:, h * KV_HEAD_W + QK_NOPE_DIM:(h + 1) * KV_HEAD_W])
        o_ref[:, h * LANES:(h + 1) * LANES] = (o * inv).astype(o_ref.dtype)

    sc, pr = {}, {}
    for h in range(MLA_HEADS + 2):
        if h < MLA_HEADS:
            sc[h] = scores(h)
        if 0 <= h - 1 < MLA_HEADS:
            pr[h - 1] = softmax(sc.pop(h - 1))
        if 0 <= h - 2 < MLA_HEADS:
            values(h - 2, *pr.pop(h - 2))


def _mla_ctx_call(q, kvx, kpe):
    return pl.pallas_call(
        _mla_ctx_body,
        grid=(BATCH,),
        in_specs=[
            pl.BlockSpec((SEQ, MLA_NOPE_W), lambda b: (b, 0)),
            pl.BlockSpec((SEQ, MLA_ROPE_W), lambda b: (b, MLA_NOPE_W // MLA_ROPE_W)),
            pl.BlockSpec((SEQ, MLA_HEADS * KV_HEAD_W), lambda b: (b, 0)),
            pl.BlockSpec((SEQ, LANES), lambda b: (b, 0)),
        ],
        out_specs=pl.BlockSpec((SEQ, MLA_HEADS * V_HEAD_DIM), lambda b: (b, 0)),
        out_shape=jax.ShapeDtypeStruct((N_CTX, MLA_HEADS * V_HEAD_DIM), BF16),
        compiler_params=_params(1),
        name="mla_attn_context",
    )(q, q, kvx, kpe)


MLA_QB = 2048
MLA_SUB = 512


def _mla_lat_body(qn_ref, qp_ref, kvl_ref, cc_ref, wkv_ref, kpl_ref, kpc_ref, o_ref):
    qlo, qhi = _half_masks_bf16(MLA_SUB)
    kvc = _dot(cc_ref[...].astype(BF16), wkv_ref[...].astype(BF16)).astype(BF16)
    kpl = kpl_ref[...].astype(BF16)
    kpc = kpc_ref[...].astype(BF16)
    qp_pair = qp_ref[...]
    chains = [(t, r) for t in range(2) for r in range(MLA_QB // MLA_SUB)]

    def scores(t, r):
        c0 = t * KV_HEAD_W
        rows = slice(r * MLA_SUB, (r + 1) * MLA_SUB)
        qcat = jnp.concatenate([qn_ref[rows, t * LANES:(t + 1) * LANES],
                                qp_pair[rows] * (qlo if t == 0 else qhi)], axis=1)
        k_lat = jnp.concatenate([kvl_ref[:, c0:c0 + QK_NOPE_DIM], kpl], axis=1)
        k_cache = jnp.concatenate([kvc[:, c0:c0 + QK_NOPE_DIM], kpc], axis=1)
        return _dot_nt(qcat, k_lat), _dot_nt(qcat, k_cache)

    def softmax(l1, l2):
        m = jnp.maximum(jnp.max(l1, axis=-1, keepdims=True), jnp.max(l2, axis=-1, keepdims=True))
        e1 = jnp.exp2((l1 - m) * MLA_EXP2_SCALE)
        e2 = jnp.exp2((l2 - m) * MLA_EXP2_SCALE)
        inv = 1.0 / (jnp.sum(e1, axis=-1, keepdims=True) + jnp.sum(e2, axis=-1, keepdims=True))
        return e1.astype(BF16), e2.astype(BF16), inv

    def values(t, r, p1, p2, inv):
        c0 = t * KV_HEAD_W
        o = (_dot(p1, kvl_ref[:, c0 + QK_NOPE_DIM:c0 + KV_HEAD_W])
             + _dot(p2, kvc[:, c0 + QK_NOPE_DIM:c0 + KV_HEAD_W]))
        o_ref[r * MLA_SUB:(r + 1) * MLA_SUB, t * LANES:(t + 1) * LANES] = (o * inv).astype(o_ref.dtype)

    n = len(chains)
    sc, pr = {}, {}
    for c in range(n + 2):
        if c < n:
            sc[c] = scores(*chains[c])
        if 0 <= c - 1 < n:
            pr[c - 1] = softmax(*sc.pop(c - 1))
        if 0 <= c - 2 < n:
            values(*chains[c - 2], *pr.pop(c - 2))


def _mla_lat_call(q, kvx, cache_ckv, w_kv_b, layer, kpe, kpe_cache):
    n_qb = DEC_SEQ // MLA_QB
    n_pairs = MLA_HEADS // 2
    lat_blk = N_CTX // DEC_SEQ
    return pl.pallas_call(
        _mla_lat_body,
        grid=(DEC_BATCH, n_pairs, n_qb),
        in_specs=[
            pl.BlockSpec((MLA_QB, 2 * QK_NOPE_DIM), lambda b, hp, qb: (N_CTX // MLA_QB + b * n_qb + qb, hp)),
            pl.BlockSpec((MLA_QB, LANES), lambda b, hp, qb: (N_CTX // MLA_QB + b * n_qb + qb, MLA_NOPE_W // LANES + hp)),
            pl.BlockSpec((DEC_SEQ, 2 * KV_HEAD_W), lambda b, hp, qb: (lat_blk + b, hp)),
            pl.BlockSpec((None, None, PAST_LEN, KV_LORA_RANK), lambda b, hp, qb: (b, layer, 0, 0)),
            pl.BlockSpec((None, KV_LORA_RANK, 2 * KV_HEAD_W), lambda b, hp, qb: (layer, 0, hp)),
            pl.BlockSpec((DEC_SEQ, LANES), lambda b, hp, qb: (lat_blk + b, 0)),
            pl.BlockSpec((None, PAST_LEN, LANES), lambda b, hp, qb: (b, 0, 0)),
        ],
        out_specs=pl.BlockSpec((MLA_QB, 2 * V_HEAD_DIM), lambda b, hp, qb: (b * n_qb + qb, hp)),
        out_shape=jax.ShapeDtypeStruct((N_LAT, MLA_HEADS * V_HEAD_DIM), BF16),
        compiler_params=_params(3),
        name="mla_attn_latent",
    )(q, q, kvx, cache_ckv, w_kv_b, kpe, kpe_cache)


def _pack_body(*refs, n_layers, col_ranges):
    ins, outs = refs[:-len(col_ranges)], refs[-len(col_ranges):]
    for o_ref, (src, lo, hi) in zip(outs, col_ranges):
        for l in range(n_layers):
            o_ref[0, l] = ins[src * n_layers + l][:, lo:hi]


def _pack_call(sources, col_ranges, name):
    n_layers = len(sources[0])
    flat = [a for src in sources for a in src]
    return pl.pallas_call(
        functools.partial(_pack_body, n_layers=n_layers, col_ranges=col_ranges),
        grid=(BATCH,),
        in_specs=[pl.BlockSpec((SEQ, a.shape[1]), lambda b: (b, 0)) for a in flat],
        out_specs=[pl.BlockSpec((1, n_layers, SEQ, hi - lo), lambda b: (b, 0, 0, 0)) for _, lo, hi in col_ranges],
        out_shape=[jax.ShapeDtypeStruct((BATCH, n_layers, SEQ, hi - lo), F32) for _, lo, hi in col_ranges],
        compiler_params=_params(1),
        name=name,
    )(*flat)


def _rope_tables():
    t = jnp.arange(DEC_SEQ, dtype=jnp.int32)
    rows, cols = t // GRID_W, t % GRID_W
    half = WIN_HEAD_DIM // 4
    freqs = ROPE_BASE ** (-jnp.arange(half, dtype=F32) / half)
    ang_r = rows.astype(F32)[:, None] * freqs[None, :]
    ang_c = cols.astype(F32)[:, None] * freqs[None, :]
    cr, sr, cc, sc = jnp.cos(ang_r), jnp.sin(ang_r), jnp.cos(ang_c), jnp.sin(ang_c)
    c64 = jnp.concatenate([cr, cr, cc, cc], axis=-1)
    s64 = jnp.concatenate([-sr, sr, -sc, sc], axis=-1)
    c_lat = jnp.tile(c64, (DEC_BATCH, LANES // WIN_HEAD_DIM))
    s_lat = jnp.tile(s64, (DEC_BATCH, LANES // WIN_HEAD_DIM))
    cos_t = jnp.concatenate([jnp.ones((N_CTX, LANES), F32), c_lat], axis=0)
    sin_t = jnp.concatenate([jnp.zeros((N_CTX, LANES), F32), s_lat], axis=0)
    return cos_t, sin_t


def kernel(x_prompt, x_sample, cache_win_k, cache_win_v, cache_mla_ckv, cache_mla_kpe, c, c_ctx, ada_w, ada_b, norm_mix, norm_ffn, win_w_qkv, win_w_o, win_sink, mla_w_in, mla_q_norm, mla_w_q_b, mla_kv_norm, mla_w_kv_b, mla_w_o, ffn_w_gate, ffn_w_up, ffn_w_down, norm_final):
    x = [x_prompt.reshape(N_CTX, D_MODEL), x_sample.reshape(N_LAT, D_MODEL)]
    cond = jnp.concatenate([c_ctx[None, :], c, jnp.zeros((MOD_ROWS - 1 - DEC_BATCH, D_MODEL), F32)], axis=0)
    mods = [_ada_call(cond, ada_w, ada_b, 0).reshape(MOD_ROWS, 1, 6 * D_MODEL)]
    tables = _rope_tables()
    cache_k = cache_win_k.reshape(DEC_BATCH, N_WIN_LAYERS, PAST_LEN, WIN_KD)
    cache_v = cache_win_v.reshape(DEC_BATCH, N_WIN_LAYERS, PAST_LEN, WIN_KD)

    kv_layers, ckv_layers, kpe_layers = [], [], []
    mix_gains = norm_mix.reshape(DEPTH, 1, D_MODEL)
    ffn_gains = norm_ffn.reshape(DEPTH, 1, D_MODEL)
    win_row_perm = [8 * p + 4 * s + t for p in range(WIN_KV_HEADS // 2) for t in range(WIN_GROUP) for s in range(2)]
    h = _modulate_call(x[0], x[1], norm_mix, 0, mods[0], 0, 1)
    for layer in range(DEPTH):
        j = layer // 2
        if layer % 2 == 0:
            q = _proj_call(h, win_w_qkv, j, 0, WIN_QD, BF16, tm=1024, tn=1024,
                           rope=(0, WIN_QD // 1024), tables=tables, pair_order=True, name="win_q_proj")
            kv = _proj_call(h, win_w_qkv, j, WIN_QD, 2 * WIN_KD, F32, tm=1024, tn=2 * WIN_KD,
                            rope=(0, 1), rope_cols=WIN_KD, tables=tables, name="win_kv_proj")
            o_ctx = _win_ctx_call(q, kv, win_sink, j)
            o_lat = _win_lat_call(q, kv, cache_k, cache_v, win_sink, j)
            w_o, row_perm = win_w_o, win_row_perm
            kv_layers.append(kv)
        else:
            cq, ckv, kpe = _mla_in_call(h, mla_w_in, j, mla_q_norm[j], mla_kv_norm[j], tables)
            wq = mla_w_q_b[j].reshape(Q_LORA_RANK, MLA_HEADS, QK_NOPE_DIM + QK_ROPE_DIM)
            wq = jnp.concatenate([wq[:, :, :QK_NOPE_DIM].reshape(Q_LORA_RANK, MLA_NOPE_W),
                                  wq[:, :, QK_NOPE_DIM:].reshape(Q_LORA_RANK, MLA_ROPE_W)], axis=1)
            q, kvx = _mla_expand_call(cq, ckv, wq, mla_w_kv_b, j, tables)
            cache_kpe = jnp.tile(cache_mla_kpe[:, j], (1, 1, LANES // QK_ROPE_DIM))
            o_ctx = _mla_ctx_call(q, kvx, kpe)
            o_lat = _mla_lat_call(q, kvx, cache_mla_ckv, mla_w_kv_b, j, kpe, cache_kpe)
            w_o, row_perm = mla_w_o, None
            ckv_layers.append(ckv)
            kpe_layers.append(kpe)
        xn, h = _resid_norm_call([o_ctx, o_lat], w_o, j, x, mods[layer], 2, ffn_gains, layer,
                                 (mods[layer], 3, 4), tm=256, row_perm=row_perm, name="mixer_out_proj")
        if layer + 1 < DEPTH:
            act, w_down, mod_next = _swiglu_call(h, ffn_w_gate, ffn_w_up, ffn_w_down, layer,
                                                 ada=(cond, ada_w, ada_b))
            mods.append(mod_next.reshape(MOD_ROWS, 1, 6 * D_MODEL))
            xn, h = _resid_norm_call([act], w_down, 0, [xn], mods[layer], 5, mix_gains, layer + 1,
                                     (mods[layer + 1], 0, 1), tm=256, name="ffn_down_proj")
            x = [xn]
        else:
            act, w_down = _swiglu_call(h, ffn_w_gate, ffn_w_up, ffn_w_down, layer)
            y_ctx, y_lat = _resid_norm_call([act], w_down, 0, [xn], mods[layer], 5,
                                            norm_final.reshape(1, 1, D_MODEL), 0, None,
                                            tm=256, name="ffn_down_final")

    y_prompt = y_ctx.reshape(BATCH, SEQ, D_MODEL)
    y_sample = y_lat.reshape(DEC_BATCH, DEC_SEQ, D_MODEL)
    new_k, new_v = _pack_call([kv_layers], [(0, 0, WIN_KD), (0, WIN_KD, 2 * WIN_KD)], "pack_win_cache")
    new_ckv, new_kpe = _pack_call([ckv_layers, kpe_layers], [(0, 0, KV_LORA_RANK), (1, 0, QK_ROPE_DIM)],
                                  "pack_mla_cache")
    head_shape = (BATCH, N_WIN_LAYERS, SEQ, WIN_KV_HEADS, WIN_HEAD_DIM)
    return (y_prompt, y_sample, new_k.reshape(head_shape), new_v.reshape(head_shape), new_ckv, new_kpe)
```
